```python
import jax, jax.numpy as jnp
from jax import lax
import numpy as np

D_MODEL = 4096
BATCH = 4
SEQ = 2048
DEPTH = 1

N_META = 16
MIX_WIDTH = D_MODEL
CONV_CH = MIX_WIDTH // 2
POOL_CH = MIX_WIDTH - CONV_CH
CONV_HEADS = 16
CONV_K = 3
POOL_WINDOWS = (2, 4, 8, 16)
N_POOL_GROUPS = len(POOL_WINDOWS)
POOL_GROUP = POOL_CH // N_POOL_GROUPS
IN_COLS = 3 * CONV_CH + POOL_CH
D_FF = 256 * ((8 * D_MODEL // 3 + 255) // 256)
LN_EPS = 1e-5
ALPHA = (2.0 * DEPTH) ** 0.25
BETA = (8.0 * DEPTH) ** -0.25

kernel_name = "hybrid_conv_pool_macaron_deepnorm"


def layer_norm(x, g, b):
    xf = x.astype(jnp.float32)
    mu = jnp.mean(xf, axis=-1, keepdims=True)
    xc = xf - mu
    var = jnp.mean(jnp.square(xc), axis=-1, keepdims=True)
    y = xc * lax.rsqrt(var + LN_EPS) * g.astype(jnp.float32) + b.astype(jnp.float32)
    return y.astype(x.dtype)


def swiglu_ffn(x, w_gu, w_down):
    gu = jnp.einsum('bld,df->blf', x, w_gu)
    gate, up = jnp.split(gu, 2, axis=-1)
    return jnp.einsum('blf,fd->bld', jax.nn.silu(gate) * up, w_down)


def causal_short_conv(z, w):
    L = z.shape[1]
    zp = jnp.pad(z, ((0, 0), (CONV_K - 1, 0), (0, 0)))
    y = zp[:, 0:L] * w[0]
    for k in range(1, CONV_K):
        y = y + zp[:, k:k + L] * w[k]
    return y


def causal_window_mean(z, window):
    L = z.shape[1]
    cs = jnp.cumsum(z, axis=1)
    prev = jnp.pad(cs, ((0, 0), (window, 0), (0, 0)))[:, :L]
    count = jnp.minimum(jnp.arange(1, L + 1), window).astype(jnp.float32)
    return (cs - prev) / count[None, :, None]


def pooling_mixer(z, pool_w, pool_scale):
    b, L, _ = z.shape
    zg = z.reshape(b, L, N_POOL_GROUPS, POOL_GROUP).astype(jnp.float32)
    pooled = jnp.stack([causal_window_mean(zg[:, :, g], POOL_WINDOWS[g])
                        for g in range(N_POOL_GROUPS)], axis=2)
    d = (pooled - zg).astype(z.dtype)
    y = jnp.einsum('blgc,gcd->blgd', d, pool_w).reshape(b, L, POOL_CH)
    return y * pool_scale


def hybrid_mixer(h, w_in, conv_w, pool_w, pool_scale, w_out):
    u = jnp.einsum('bld,dc->blc', h, w_in)
    gate_b = u[..., 0:CONV_CH]
    gate_c = u[..., CONV_CH:2 * CONV_CH]
    x_in = u[..., 2 * CONV_CH:3 * CONV_CH]
    z_pool = u[..., 3 * CONV_CH:]
    y_conv = gate_b * causal_short_conv(gate_c * x_in, conv_w)
    y_pool = pooling_mixer(z_pool, pool_w, pool_scale)
    y = jnp.concatenate([y_conv, y_pool], axis=-1)
    return jnp.einsum('blc,cd->bld', y, w_out)


def setup_inputs(seed: int = 0) -> dict:
    key = jax.random.key(seed)
    ks = jax.random.split(key, 20)
    f32 = jnp.float32
    D, F = D_MODEL, D_FF

    def nrm(k, shape, scale):
        return jax.random.normal(k, shape, f32) * scale

    def gain(k):
        return 1.0 + 0.05 * jax.random.normal(k, (DEPTH, D), f32)

    def bias(k):
        return 0.02 * jax.random.normal(k, (DEPTH, D), f32)

    return {
        "x": jax.random.normal(ks[0], (BATCH, SEQ, D), f32),
        "meta_tokens": nrm(ks[1], (N_META, D), 1.0),
        "ffn1_w_gu": nrm(ks[2], (DEPTH, D, 2 * F), D ** -0.5),
        "ffn1_w_down": nrm(ks[3], (DEPTH, F, D), BETA * F ** -0.5),
        "ln1_g": gain(ks[4]),
        "ln1_b": bias(ks[5]),
        "w_in": nrm(ks[6], (DEPTH, D, IN_COLS), D ** -0.5),
        "conv_w": nrm(ks[7], (DEPTH, CONV_K, CONV_CH), CONV_K ** -0.5),
        "pool_w": nrm(ks[8], (DEPTH, N_POOL_GROUPS, POOL_GROUP, POOL_GROUP), POOL_GROUP ** -0.5),
        "pool_scale": 1.0 + 0.1 * jax.random.normal(ks[9], (DEPTH, POOL_CH), f32),
        "w_out": nrm(ks[10], (DEPTH, MIX_WIDTH, D), BETA * MIX_WIDTH ** -0.5),
        "ln2_g": gain(ks[11]),
        "ln2_b": bias(ks[12]),
        "ffn2_w_gu": nrm(ks[13], (DEPTH, D, 2 * F), D ** -0.5),
        "ffn2_w_down": nrm(ks[14], (DEPTH, F, D), BETA * F ** -0.5),
        "ln3_g": gain(ks[15]),
        "ln3_b": bias(ks[16]),
    }


def reference(x, meta_tokens, ffn1_w_gu, ffn1_w_down, ln1_g, ln1_b, w_in, conv_w, pool_w,
              pool_scale, w_out, ln2_g, ln2_b, ffn2_w_gu, ffn2_w_down, ln3_g, ln3_b):
    b = x.shape[0]
    meta = jnp.broadcast_to(meta_tokens.astype(x.dtype)[None], (b, N_META, D_MODEL))
    h = jnp.concatenate([meta, x], axis=1)
    for i in range(DEPTH):
        h = layer_norm(ALPHA * h + 0.5 * swiglu_ffn(h, ffn1_w_gu[i], ffn1_w_down[i]),
                       ln1_g[i], ln1_b[i])
        h = layer_norm(ALPHA * h + hybrid_mixer(h, w_in[i], conv_w[i], pool_w[i],
                                                 pool_scale[i], w_out[i]),
                       ln2_g[i], ln2_b[i])
        h = layer_norm(ALPHA * h + 0.5 * swiglu_ffn(h, ffn2_w_gu[i], ffn2_w_down[i]),
                       ln3_g[i], ln3_b[i])
    return h[:, N_META:]
```

```python
import functools

import jax
import jax.numpy as jnp
from jax import lax
from jax.experimental import pallas as pl
from jax.experimental.pallas import tpu as pltpu

N_META = 16
CONV_K = 3
POOL_WINDOWS = (2, 4, 8, 16)
LN_EPS = 1e-5
DEPTH = 1
ALPHA = (2.0 * DEPTH) ** 0.25

BF16 = jnp.bfloat16
F32 = jnp.float32

V7X_MXU_COLS = 256
V7X_VMEM_LIMIT_BYTES = 60 * 1024 * 1024
HALO = 16
LN_ROWS = 32
FFN_DOWN_COLS = 512


def _layer_norm_rows(y, g, b):
    mu = jnp.mean(y, axis=-1, keepdims=True)
    yc = y - mu
    var = jnp.mean(yc * yc, axis=-1, keepdims=True)
    return yc * lax.rsqrt(var + LN_EPS) * g + b


def _residual_layer_norm(o_ref, res_ref, branch_scale, g_ref, b_ref):
    rows = o_ref.shape[0]
    step = min(LN_ROWS, rows)

    def body(r, carry):
        r0 = pl.multiple_of(r * step, step)
        y = o_ref[pl.ds(r0, step), :]
        if res_ref is not None:
            y = ALPHA * res_ref[pl.ds(r0, step), :] + branch_scale * y
        o_ref[pl.ds(r0, step), :] = _layer_norm_rows(y, g_ref[...], b_ref[...])
        return carry

    lax.fori_loop(0, rows // step, body, 0)


def _ffn_kernel(x_ref, wg_ref, wu_ref, wd_ref, g_ref, b_ref, o_ref, xb_ref):
    j = pl.program_id(1)

    @pl.when(j == 0)
    def _():
        xb_ref[...] = x_ref[...].astype(BF16)
        o_ref[...] = jnp.zeros_like(o_ref)

    xb = xb_ref[...]
    gate = jnp.dot(xb, wg_ref[...], preferred_element_type=F32)
    up = jnp.dot(xb, wu_ref[...], preferred_element_type=F32)
    act = (gate * jax.nn.sigmoid(gate) * up).astype(BF16)
    d = o_ref.shape[1]
    for n0 in range(0, d, FFN_DOWN_COLS):
        cols = slice(n0, n0 + FFN_DOWN_COLS)
        o_ref[:, cols] += jnp.dot(act, wd_ref[:, cols], preferred_element_type=F32)

    @pl.when(j == pl.num_programs(1) - 1)
    def _():
        _residual_layer_norm(o_ref, x_ref, 0.5, g_ref, b_ref)


def _ffn_ln(x, w_gu, w_down, ln_g, ln_b, *, tm):
    m, d = x.shape
    f = w_down.shape[0]
    tf = V7X_MXU_COLS
    assert m % tm == 0 and f % tf == 0
    nf = f // tf
    return pl.pallas_call(
        _ffn_kernel,
        grid=(m // tm, nf),
        in_specs=[
            pl.BlockSpec((tm, d), lambda i, j: (i, 0)),
            pl.BlockSpec((d, tf), lambda i, j: (0, j)),
            pl.BlockSpec((d, tf), lambda i, j: (0, nf + j)),
            pl.BlockSpec((tf, d), lambda i, j: (j, 0)),
            pl.BlockSpec((1, d), lambda i, j: (0, 0)),
            pl.BlockSpec((1, d), lambda i, j: (0, 0)),
        ],
        out_specs=pl.BlockSpec((tm, d), lambda i, j: (i, 0)),
        out_shape=jax.ShapeDtypeStruct((m, d), F32),
        scratch_shapes=[pltpu.VMEM((tm, d), BF16)],
        compiler_params=pltpu.CompilerParams(
            dimension_semantics=("arbitrary", "arbitrary"),
            vmem_limit_bytes=V7X_VMEM_LIMIT_BYTES,
        ),
        name="ffn_ln",
    )(x, w_gu, w_gu, w_down, ln_g, ln_b)


def _assemble_lhs(hb_ref, main_ref, halo_ref, meta_ref, tiles_per_seq):
    first = (pl.program_id(0) % tiles_per_seq) == 0

    @pl.when(first)
    def _():
        hb_ref[0:HALO, :] = meta_ref[...].astype(BF16)

    @pl.when(jnp.logical_not(first))
    def _():
        hb_ref[0:HALO, :] = halo_ref[...].astype(BF16)

    hb_ref[HALO:, :] = main_ref[...].astype(BF16)


def _conv_kernel(main_ref, halo_ref, meta_ref, wb_ref, wc_ref, wx_ref, cw_ref, y_ref, hb_ref,
                 *, tiles_per_seq):
    @pl.when(pl.program_id(1) == 0)
    def _():
        _assemble_lhs(hb_ref, main_ref, halo_ref, meta_ref, tiles_per_seq)

    hb = hb_ref[...]
    gate_b = jnp.dot(hb, wb_ref[...], preferred_element_type=F32)
    gate_c = jnp.dot(hb, wc_ref[...], preferred_element_type=F32)
    x_in = jnp.dot(hb, wx_ref[...], preferred_element_type=F32)
    v = gate_c * x_in
    conv = pltpu.roll(v, 2, 0) * cw_ref[0:1, :]
    conv = conv + pltpu.roll(v, 1, 0) * cw_ref[1:2, :]
    conv = conv + v * cw_ref[2:3, :]
    y_ref[...] = (gate_b * conv)[HALO:, :].astype(BF16)


def _pool_kernel(main_ref, halo_ref, meta_ref, wp_ref, pw_ref, ps_ref, y_ref, hb_ref,
                 *, tiles_per_seq):
    g = pl.program_id(1)

    @pl.when(g == 0)
    def _():
        _assemble_lhs(hb_ref, main_ref, halo_ref, meta_ref, tiles_per_seq)

    z = jnp.dot(hb_ref[...], wp_ref[...], preferred_element_type=F32)
    s = z
    for level in range(len(POOL_WINDOWS)):
        doubled = s + pltpu.roll(s, 1 << level, 0)
        s = doubled if level == 0 else jnp.where(level <= g, doubled, s)
    inv_w = jnp.where(g == 0, 1.0 / POOL_WINDOWS[0],
                      jnp.where(g == 1, 1.0 / POOL_WINDOWS[1],
                                jnp.where(g == 2, 1.0 / POOL_WINDOWS[2], 1.0 / POOL_WINDOWS[3])))
    diff = (s * inv_w.astype(F32) - z)[HALO:, :].astype(BF16)
    y = jnp.dot(diff, pw_ref[...], preferred_element_type=F32) * ps_ref[...]
    y_ref[...] = y.astype(BF16)


def _mixer_inputs(h, *, ts, seq):
    m, d = h.shape
    tiles_per_seq = seq // ts
    halo_blocks_per_tile = ts // HALO
    specs = [
        pl.BlockSpec((ts, d), lambda t, c: (t, 0)),
        pl.BlockSpec((HALO, d), lambda t, c: (jnp.maximum(t * halo_blocks_per_tile - 1, 0), 0)),
        pl.BlockSpec((HALO, d), lambda t, c: (0, 0)),
    ]
    return m // ts, tiles_per_seq, specs


def _conv_mixer(h, meta_h, w_in, conv_w, *, ts, seq):
    d = h.shape[1]
    cc = conv_w.shape[1]
    tc = V7X_MXU_COLS
    nc = cc // tc
    n_tiles, tiles_per_seq, lhs_specs = _mixer_inputs(h, ts=ts, seq=seq)
    return pl.pallas_call(
        functools.partial(_conv_kernel, tiles_per_seq=tiles_per_seq),
        grid=(n_tiles, nc),
        in_specs=lhs_specs + [
            pl.BlockSpec((d, tc), lambda t, c: (0, c)),
            pl.BlockSpec((d, tc), lambda t, c: (0, nc + c)),
            pl.BlockSpec((d, tc), lambda t, c: (0, 2 * nc + c)),
            pl.BlockSpec((CONV_K, tc), lambda t, c: (0, c)),
        ],
        out_specs=pl.BlockSpec((ts, tc), lambda t, c: (t, c)),
        out_shape=jax.ShapeDtypeStruct((h.shape[0], cc), BF16),
        scratch_shapes=[pltpu.VMEM((ts + HALO, d), BF16)],
        compiler_params=pltpu.CompilerParams(
            dimension_semantics=("arbitrary", "arbitrary"),
            vmem_limit_bytes=V7X_VMEM_LIMIT_BYTES,
        ),
        name="mixer_conv",
    )(h, h, meta_h, w_in, w_in, w_in, conv_w)


def _pool_mixer(h, meta_h, w_in, pool_w, pool_scale, *, ts, seq):
    d = h.shape[1]
    n_groups, pg, _ = pool_w.shape
    first_pool_block = (w_in.shape[1] - n_groups * pg) // pg
    n_tiles, tiles_per_seq, lhs_specs = _mixer_inputs(h, ts=ts, seq=seq)
    return pl.pallas_call(
        functools.partial(_pool_kernel, tiles_per_seq=tiles_per_seq),
        grid=(n_tiles, n_groups),
        in_specs=lhs_specs + [
            pl.BlockSpec((d, pg), lambda t, g: (0, first_pool_block + g)),
            pl.BlockSpec((None, pg, pg), lambda t, g: (g, 0, 0)),
            pl.BlockSpec((1, pg), lambda t, g: (0, g)),
        ],
        out_specs=pl.BlockSpec((ts, pg), lambda t, g: (t, g)),
        out_shape=jax.ShapeDtypeStruct((h.shape[0], n_groups * pg), BF16),
        scratch_shapes=[pltpu.VMEM((ts + HALO, d), BF16)],
        compiler_params=pltpu.CompilerParams(
            dimension_semantics=("arbitrary", "arbitrary"),
            vmem_limit_bytes=V7X_VMEM_LIMIT_BYTES,
        ),
        name="mixer_pool",
    )(h, h, meta_h, w_in, pool_w, pool_scale)


def _out_kernel(yc_ref, yp_ref, wc_ref, wp_ref, h_ref, g_ref, b_ref, o_ref, *, tn):
    n = pl.program_id(1)
    acc = jnp.dot(yc_ref[...], wc_ref[...], preferred_element_type=F32)
    acc = acc + jnp.dot(yp_ref[...], wp_ref[...], preferred_element_type=F32)
    col = pl.multiple_of(n * tn, tn)
    o_ref[:, pl.ds(col, tn)] = ALPHA * h_ref[...] + acc

    @pl.when(n == pl.num_programs(1) - 1)
    def _():
        _residual_layer_norm(o_ref, None, None, g_ref, b_ref)


def _out_proj_ln(yc, yp, w_out, h, ln_g, ln_b, *, tm, tn):
    m, d = h.shape
    cc = yc.shape[1]
    pc = yp.shape[1]
    assert cc == pc and m % tm == 0 and d % tn == 0
    return pl.pallas_call(
        functools.partial(_out_kernel, tn=tn),
        grid=(m // tm, d // tn),
        in_specs=[
            pl.BlockSpec((tm, cc), lambda i, n: (i, 0)),
            pl.BlockSpec((tm, pc), lambda i, n: (i, 0)),
            pl.BlockSpec((cc, tn), lambda i, n: (0, n)),
            pl.BlockSpec((pc, tn), lambda i, n: (1, n)),
            pl.BlockSpec((tm, tn), lambda i, n: (i, n)),
            pl.BlockSpec((1, d), lambda i, n: (0, 0)),
            pl.BlockSpec((1, d), lambda i, n: (0, 0)),
        ],
        out_specs=pl.BlockSpec((tm, d), lambda i, n: (i, 0)),
        out_shape=jax.ShapeDtypeStruct((m, d), F32),
        compiler_params=pltpu.CompilerParams(
            dimension_semantics=("arbitrary", "arbitrary"),
            vmem_limit_bytes=V7X_VMEM_LIMIT_BYTES,
        ),
        name="mixer_out_ln",
    )(yc, yp, w_out, w_out, h, ln_g, ln_b)


def kernel(x, meta_tokens, ffn1_w_gu, ffn1_w_down, ln1_g, ln1_b, w_in, conv_w, pool_w, pool_scale,
           w_out, ln2_g, ln2_b, ffn2_w_gu, ffn2_w_down, ln3_g, ln3_b):
    b, seq, d = x.shape
    assert meta_tokens.shape == (N_META, d) and N_META == HALO
    assert ffn1_w_gu.shape[0] == DEPTH
    tm = 512

    h = x.reshape(b * seq, d)
    meta = meta_tokens.astype(x.dtype)
    for i in range(DEPTH):
        wgu1, wd1 = ffn1_w_gu[i].astype(BF16), ffn1_w_down[i].astype(BF16)
        wgu2, wd2 = ffn2_w_gu[i].astype(BF16), ffn2_w_down[i].astype(BF16)
        w_in_b, w_out_b, pool_w_b = w_in[i].astype(BF16), w_out[i].astype(BF16), pool_w[i].astype(BF16)
        g1, b1 = ln1_g[i][None].astype(F32), ln1_b[i][None].astype(F32)
        g2, b2 = ln2_g[i][None].astype(F32), ln2_b[i][None].astype(F32)
        g3, b3 = ln3_g[i][None].astype(F32), ln3_b[i][None].astype(F32)

        h1 = _ffn_ln(h, wgu1, wd1, g1, b1, tm=tm)
        meta1 = _ffn_ln(meta, wgu1, wd1, g1, b1, tm=N_META)
        y_conv = _conv_mixer(h1, meta1, w_in_b, conv_w[i], ts=tm, seq=seq)
        y_pool = _pool_mixer(h1, meta1, w_in_b, pool_w_b, pool_scale[i][None], ts=tm, seq=seq)
        h2 = _out_proj_ln(y_conv, y_pool, w_out_b, h1, g2, b2, tm=tm, tn=512)
        h = _ffn_ln(h2, wgu2, wd2, g3, b3, tm=tm)
        assert DEPTH == 1
    return h.reshape(b, seq, d)
```

```python
import functools
from typing import NamedTuple

import jax
import jax.numpy as jnp
from jax import lax
from jax.experimental import pallas as pl
from jax.experimental.pallas import tpu as pltpu

N_META = 16
CONV_K = 3
POOL_WINDOWS = (2, 4, 8, 16)
LN_EPS = 1e-5
DEPTH = 1
ALPHA = (2.0 * DEPTH) ** 0.25

BF16 = jnp.bfloat16
F32 = jnp.float32

V7X_MXU_COLS = 256
V7X_VMEM_LIMIT_BYTES = 60 * 1024 * 1024
HALO = 16
LN_ROWS = 32
FFN_DOWN_COLS = 512


def _layer_norm_rows(y, g, b):
    mu = jnp.mean(y, axis=-1, keepdims=True)
    yc = y - mu
    var = jnp.mean(yc * yc, axis=-1, keepdims=True)
    return yc * lax.rsqrt(var + LN_EPS) * g + b


def _residual_layer_norm(o_ref, res_ref, branch_scale, g_ref, b_ref):
    rows = o_ref.shape[0]
    step = min(LN_ROWS, rows)

    def body(r, carry):
        r0 = pl.multiple_of(r * step, step)
        y = o_ref[pl.ds(r0, step), :]
        if res_ref is not None:
            y = ALPHA * res_ref[pl.ds(r0, step), :] + branch_scale * y
        o_ref[pl.ds(r0, step), :] = _layer_norm_rows(y, g_ref[...], b_ref[...])
        return carry

    lax.fori_loop(0, rows // step, body, 0)


def _swiglu_chunk(xb, wg, wu):
    gate = jnp.dot(xb, wg, preferred_element_type=F32)
    up = jnp.dot(xb, wu, preferred_element_type=F32)
    return (gate * jax.nn.sigmoid(gate) * up).astype(BF16)


class _CastJob(NamedTuple):
    src: jax.Array
    block: tuple
    start: int

    @property
    def n_blocks(self):
        return (self.src.shape[0] // self.block[0]) * (self.src.shape[1] // self.block[1])

    def spec(self, steps_per_row_tile):
        col_blocks = self.src.shape[1] // self.block[1]

        def index_map(i, j):
            s = jnp.clip(i * steps_per_row_tile + j - self.start, 0, self.n_blocks - 1)
            return (s // col_blocks, s % col_blocks)

        return pl.BlockSpec(self.block, index_map)


def _ffn_kernel(*refs, cast_ranges):
    n_cast = len(cast_ranges)
    x_ref, wg_ref, wu_ref, wd_ref, g_ref, b_ref = refs[:6]
    cast_in = refs[6:6 + n_cast]
    n_in = len(refs) - (2 + n_cast)
    o_ref = refs[n_in]
    cast_out = refs[n_in + 1:n_in + 1 + n_cast]
    xb_ref = refs[n_in + 1 + n_cast]
    j = pl.program_id(1)
    step = pl.program_id(0) * pl.num_programs(1) + j

    @pl.when(j == 0)
    def _():
        xb_ref[...] = x_ref[...].astype(BF16)
        o_ref[...] = jnp.zeros_like(o_ref)

    for src_ref, dst_ref, (start, stop) in zip(cast_in, cast_out, cast_ranges):
        @pl.when(jnp.logical_and(step >= start, step < stop))
        def _(src_ref=src_ref, dst_ref=dst_ref):
            dst_ref[...] = src_ref[...].astype(BF16)

    act = _swiglu_chunk(xb_ref[...], wg_ref[...], wu_ref[...])
    d = o_ref.shape[1]
    for n0 in range(0, d, FFN_DOWN_COLS):
        cols = slice(n0, n0 + FFN_DOWN_COLS)
        o_ref[:, cols] += jnp.dot(act, wd_ref[:, cols], preferred_element_type=F32)

    @pl.when(j == pl.num_programs(1) - 1)
    def _():
        _residual_layer_norm(o_ref, x_ref, 0.5, g_ref, b_ref)


def _ffn_ln(x, wg, wu, up_block_offset, w_down, ln_g, ln_b, *, tm, out_init=None, cast_jobs=()):
    m, d = x.shape
    f = w_down.shape[0]
    tf = V7X_MXU_COLS
    assert m % tm == 0 and f % tf == 0
    nf = f // tf
    first_tile = 0 if out_init is None else 1
    row_tiles = m // tm - first_tile
    for job in cast_jobs:
        assert job.start + job.n_blocks <= row_tiles * nf
    row_spec = pl.BlockSpec((tm, d), lambda i, j: (i + first_tile, 0))
    operands = [x, wg, wu, w_down, ln_g, ln_b] + [job.src for job in cast_jobs]
    if out_init is not None:
        operands.append(out_init)
    outs = pl.pallas_call(
        functools.partial(
            _ffn_kernel,
            cast_ranges=tuple((job.start, job.start + job.n_blocks) for job in cast_jobs)),
        grid=(row_tiles, nf),
        in_specs=[
            row_spec,
            pl.BlockSpec((d, tf), lambda i, j: (0, j)),
            pl.BlockSpec((d, tf), lambda i, j: (0, up_block_offset + j)),
            pl.BlockSpec((tf, d), lambda i, j: (j, 0)),
            pl.BlockSpec((1, d), lambda i, j: (0, 0)),
            pl.BlockSpec((1, d), lambda i, j: (0, 0)),
        ] + [job.spec(nf) for job in cast_jobs]
        + ([pl.BlockSpec(memory_space=pl.ANY)] if out_init is not None else []),
        out_specs=[row_spec] + [job.spec(nf) for job in cast_jobs],
        out_shape=[jax.ShapeDtypeStruct((m, d), F32)]
        + [jax.ShapeDtypeStruct(job.src.shape, BF16) for job in cast_jobs],
        input_output_aliases={} if out_init is None else {len(operands) - 1: 0},
        scratch_shapes=[pltpu.VMEM((tm, d), BF16)],
        compiler_params=pltpu.CompilerParams(
            dimension_semantics=("arbitrary", "arbitrary"),
            vmem_limit_bytes=V7X_VMEM_LIMIT_BYTES,
        ),
        name="ffn_ln",
    )(*operands)
    return outs[0], tuple(outs[1:])


def _ffn_head_kernel(x_ref, meta_ref, wg_ref, wu_ref, wd_ref, g_ref, b_ref,
                     o_ref, om_ref, wgb_ref, wub_ref, wdb_ref, xb_ref):
    j = pl.program_id(0)
    n_meta = meta_ref.shape[0]

    @pl.when(j == 0)
    def _():
        xb_ref[0:n_meta, :] = meta_ref[...].astype(BF16)
        xb_ref[n_meta:, :] = x_ref[...].astype(BF16)
        o_ref[...] = jnp.zeros_like(o_ref)
        om_ref[...] = jnp.zeros_like(om_ref)

    wgb_ref[...] = wg_ref[...].astype(BF16)
    wub_ref[...] = wu_ref[...].astype(BF16)
    wdb_ref[...] = wd_ref[...].astype(BF16)
    act = _swiglu_chunk(xb_ref[...], wgb_ref[...], wub_ref[...])
    d = o_ref.shape[1]
    for n0 in range(0, d, FFN_DOWN_COLS):
        cols = slice(n0, n0 + FFN_DOWN_COLS)
        part = jnp.dot(act, wdb_ref[:, cols], preferred_element_type=F32)
        om_ref[:, cols] += part[0:n_meta, :]
        o_ref[:, cols] += part[n_meta:, :]

    @pl.when(j == pl.num_programs(0) - 1)
    def _():
        _residual_layer_norm(om_ref, meta_ref, 0.5, g_ref, b_ref)
        _residual_layer_norm(o_ref, x_ref, 0.5, g_ref, b_ref)


def _ffn_ln_head(x, meta, w_gu, w_down, ln_g, ln_b, *, tm):
    m, d = x.shape
    n_meta = meta.shape[0]
    f = w_down.shape[0]
    tf = V7X_MXU_COLS
    nf = f // tf
    resident = pl.Buffered(1)
    return pl.pallas_call(
        _ffn_head_kernel,
        grid=(nf,),
        in_specs=[
            pl.BlockSpec((tm, d), lambda j: (0, 0), pipeline_mode=resident),
            pl.BlockSpec((n_meta, d), lambda j: (0, 0), pipeline_mode=resident),
            pl.BlockSpec((d, tf), lambda j: (0, j)),
            pl.BlockSpec((d, tf), lambda j: (0, nf + j)),
            pl.BlockSpec((tf, d), lambda j: (j, 0)),
            pl.BlockSpec((1, d), lambda j: (0, 0)),
            pl.BlockSpec((1, d), lambda j: (0, 0)),
        ],
        out_specs=[
            pl.BlockSpec((tm, d), lambda j: (0, 0), pipeline_mode=resident),
            pl.BlockSpec((n_meta, d), lambda j: (0, 0), pipeline_mode=resident),
            pl.BlockSpec((d, tf), lambda j: (0, j)),
            pl.BlockSpec((d, tf), lambda j: (0, j)),
            pl.BlockSpec((tf, d), lambda j: (j, 0)),
        ],
        out_shape=[
            jax.ShapeDtypeStruct((m, d), F32),
            jax.ShapeDtypeStruct((n_meta, d), F32),
            jax.ShapeDtypeStruct((d, f), BF16),
            jax.ShapeDtypeStruct((d, f), BF16),
            jax.ShapeDtypeStruct((f, d), BF16),
        ],
        scratch_shapes=[pltpu.VMEM((n_meta + tm, d), BF16)],
        compiler_params=pltpu.CompilerParams(
            dimension_semantics=("arbitrary",),
            vmem_limit_bytes=V7X_VMEM_LIMIT_BYTES,
        ),
        name="ffn_ln_head",
    )(x, meta, w_gu, w_gu, w_down, ln_g, ln_b)


def _assemble_lhs(hb_ref, main_ref, halo_ref, meta_ref, tiles_per_seq):
    first = (pl.program_id(0) % tiles_per_seq) == 0

    @pl.when(first)
    def _():
        hb_ref[0:HALO, :] = meta_ref[...].astype(BF16)

    @pl.when(jnp.logical_not(first))
    def _():
        hb_ref[0:HALO, :] = halo_ref[...].astype(BF16)

    hb_ref[HALO:, :] = main_ref[...].astype(BF16)


def _conv_kernel(main_ref, halo_ref, meta_ref, wb_ref, wc_ref, wx_ref, cw_ref, y_ref, hb_ref,
                 *, tiles_per_seq):
    @pl.when(pl.program_id(1) == 0)
    def _():
        _assemble_lhs(hb_ref, main_ref, halo_ref, meta_ref, tiles_per_seq)

    hb = hb_ref[...]
    gate_b = jnp.dot(hb, wb_ref[...], preferred_element_type=F32)
    gate_c = jnp.dot(hb, wc_ref[...], preferred_element_type=F32)
    x_in = jnp.dot(hb, wx_ref[...], preferred_element_type=F32)
    v = gate_c * x_in
    conv = pltpu.roll(v, 2, 0) * cw_ref[0:1, :]
    conv = conv + pltpu.roll(v, 1, 0) * cw_ref[1:2, :]
    conv = conv + v * cw_ref[2:3, :]
    y_ref[...] = (gate_b * conv)[HALO:, :].astype(BF16)


def _pool_kernel(main_ref, halo_ref, meta_ref, wp_ref, pw_ref, ps_ref, y_ref, hb_ref,
                 *, tiles_per_seq):
    g = pl.program_id(1)

    @pl.when(g == 0)
    def _():
        _assemble_lhs(hb_ref, main_ref, halo_ref, meta_ref, tiles_per_seq)

    z = jnp.dot(hb_ref[...], wp_ref[...], preferred_element_type=F32)
    s = z
    for level in range(len(POOL_WINDOWS)):
        doubled = s + pltpu.roll(s, 1 << level, 0)
        s = doubled if level == 0 else jnp.where(level <= g, doubled, s)
    inv_w = jnp.where(g == 0, 1.0 / POOL_WINDOWS[0],
                      jnp.where(g == 1, 1.0 / POOL_WINDOWS[1],
                                jnp.where(g == 2, 1.0 / POOL_WINDOWS[2], 1.0 / POOL_WINDOWS[3])))
    diff = (s * inv_w.astype(F32) - z)[HALO:, :].astype(BF16)
    y = jnp.dot(diff, pw_ref[...], preferred_element_type=F32) * ps_ref[...]
    y_ref[...] = y.astype(BF16)


def _mixer_inputs(h, *, ts, seq):
    m, d = h.shape
    tiles_per_seq = seq // ts
    halo_blocks_per_tile = ts // HALO
    specs = [
        pl.BlockSpec((ts, d), lambda t, c: (t, 0)),
        pl.BlockSpec((HALO, d), lambda t, c: (jnp.maximum(t * halo_blocks_per_tile - 1, 0), 0)),
        pl.BlockSpec((HALO, d), lambda t, c: (0, 0)),
    ]
    return m // ts, tiles_per_seq, specs


def _conv_mixer(h, meta_h, w_in, conv_w, *, ts, seq):
    d = h.shape[1]
    cc = conv_w.shape[1]
    tc = V7X_MXU_COLS
    nc = cc // tc
    n_tiles, tiles_per_seq, lhs_specs = _mixer_inputs(h, ts=ts, seq=seq)
    return pl.pallas_call(
        functools.partial(_conv_kernel, tiles_per_seq=tiles_per_seq),
        grid=(n_tiles, nc),
        in_specs=lhs_specs + [
            pl.BlockSpec((d, tc), lambda t, c: (0, c)),
            pl.BlockSpec((d, tc), lambda t, c: (0, nc + c)),
            pl.BlockSpec((d, tc), lambda t, c: (0, 2 * nc + c)),
            pl.BlockSpec((CONV_K, tc), lambda t, c: (0, c)),
        ],
        out_specs=pl.BlockSpec((ts, tc), lambda t, c: (t, c)),
        out_shape=jax.ShapeDtypeStruct((h.shape[0], cc), BF16),
        scratch_shapes=[pltpu.VMEM((ts + HALO, d), BF16)],
        compiler_params=pltpu.CompilerParams(
            dimension_semantics=("arbitrary", "arbitrary"),
            vmem_limit_bytes=V7X_VMEM_LIMIT_BYTES,
        ),
        name="mixer_conv",
    )(h, h, meta_h, w_in, w_in, w_in, conv_w)


def _pool_mixer(h, meta_h, w_in, pool_w, pool_scale, *, ts, seq):
    d = h.shape[1]
    n_groups, pg, _ = pool_w.shape
    first_pool_block = (w_in.shape[1] - n_groups * pg) // pg
    n_tiles, tiles_per_seq, lhs_specs = _mixer_inputs(h, ts=ts, seq=seq)
    return pl.pallas_call(
        functools.partial(_pool_kernel, tiles_per_seq=tiles_per_seq),
        grid=(n_tiles, n_groups),
        in_specs=lhs_specs + [
            pl.BlockSpec((d, pg), lambda t, g: (0, first_pool_block + g)),
            pl.BlockSpec((None, pg, pg), lambda t, g: (g, 0, 0)),
            pl.BlockSpec((1, pg), lambda t, g: (0, g)),
        ],
        out_specs=pl.BlockSpec((ts, pg), lambda t, g: (t, g)),
        out_shape=jax.ShapeDtypeStruct((h.shape[0], n_groups * pg), BF16),
        scratch_shapes=[pltpu.VMEM((ts + HALO, d), BF16)],
        compiler_params=pltpu.CompilerParams(
            dimension_semantics=("arbitrary", "arbitrary"),
            vmem_limit_bytes=V7X_VMEM_LIMIT_BYTES,
        ),
        name="mixer_pool",
    )(h, h, meta_h, w_in, pool_w, pool_scale)


def _out_kernel(yc_ref, yp_ref, wc_ref, wp_ref, h_ref, g_ref, b_ref, o_ref, *, tn):
    n = pl.program_id(1)
    acc = jnp.dot(yc_ref[...], wc_ref[...], preferred_element_type=F32)
    acc = acc + jnp.dot(yp_ref[...], wp_ref[...], preferred_element_type=F32)
    col = pl.multiple_of(n * tn, tn)
    o_ref[:, pl.ds(col, tn)] = ALPHA * h_ref[...] + acc

    @pl.when(n == pl.num_programs(1) - 1)
    def _():
        _residual_layer_norm(o_ref, None, None, g_ref, b_ref)


def _out_proj_ln(yc, yp, w_out, h, ln_g, ln_b, *, tm, tn):
    m, d = h.shape
    cc = yc.shape[1]
    pc = yp.shape[1]
    assert cc == pc and m % tm == 0 and d % tn == 0
    return pl.pallas_call(
        functools.partial(_out_kernel, tn=tn),
        grid=(m // tm, d // tn),
        in_specs=[
            pl.BlockSpec((tm, cc), lambda i, n: (i, 0)),
            pl.BlockSpec((tm, pc), lambda i, n: (i, 0)),
            pl.BlockSpec((cc, tn), lambda i, n: (0, n)),
            pl.BlockSpec((pc, tn), lambda i, n: (1, n)),
            pl.BlockSpec((tm, tn), lambda i, n: (i, n)),
            pl.BlockSpec((1, d), lambda i, n: (0, 0)),
            pl.BlockSpec((1, d), lambda i, n: (0, 0)),
        ],
        out_specs=pl.BlockSpec((tm, d), lambda i, n: (i, 0)),
        out_shape=jax.ShapeDtypeStruct((m, d), F32),
        compiler_params=pltpu.CompilerParams(
            dimension_semantics=("arbitrary", "arbitrary"),
            vmem_limit_bytes=V7X_VMEM_LIMIT_BYTES,
        ),
        name="mixer_out_ln",
    )(yc, yp, w_out, w_out, h, ln_g, ln_b)


def kernel(x, meta_tokens, ffn1_w_gu, ffn1_w_down, ln1_g, ln1_b, w_in, conv_w, pool_w, pool_scale,
           w_out, ln2_g, ln2_b, ffn2_w_gu, ffn2_w_down, ln3_g, ln3_b):
    b, seq, d = x.shape
    assert meta_tokens.shape == (N_META, d) and N_META == HALO
    assert ffn1_w_gu.shape[0] == DEPTH
    tm = 512

    h = x.reshape(b * seq, d)
    meta = meta_tokens.astype(x.dtype)
    for i in range(DEPTH):
        g1, b1 = ln1_g[i][None].astype(F32), ln1_b[i][None].astype(F32)
        g2, b2 = ln2_g[i][None].astype(F32), ln2_b[i][None].astype(F32)
        g3, b3 = ln3_g[i][None].astype(F32), ln3_b[i][None].astype(F32)
        n_groups, pg, _ = pool_w[i].shape
        nf = ffn1_w_down[i].shape[0] // V7X_MXU_COLS

        h1_head, meta1, wg1, wu1, wd1 = _ffn_ln_head(
            h, meta, ffn1_w_gu[i], ffn1_w_down[i], g1, b1, tm=tm)
        jobs = (
            _CastJob(ffn2_w_gu[i], (16, ffn2_w_gu[i].shape[1] // 2), 0),
            _CastJob(ffn2_w_down[i], (32, d), 0),
            _CastJob(w_out[i], (32, d), ffn2_w_down[i].shape[0] // 32),
            _CastJob(w_in[i], (16, w_in[i].shape[1]), 0),
            _CastJob(pool_w[i].reshape(n_groups * pg, pg), (128, pg), w_in[i].shape[0] // 16),
        )
        h1, (wgu2, wd2, w_out_b, w_in_b, pool_w_b) = _ffn_ln(
            h, wg1, wu1, 0, wd1, g1, b1, tm=tm, out_init=h1_head, cast_jobs=jobs)
        pool_w_b = pool_w_b.reshape(n_groups, pg, pg)

        y_conv = _conv_mixer(h1, meta1, w_in_b, conv_w[i], ts=tm, seq=seq)
        y_pool = _pool_mixer(h1, meta1, w_in_b, pool_w_b, pool_scale[i][None], ts=tm, seq=seq)
        h2 = _out_proj_ln(y_conv, y_pool, w_out_b, h1, g2, b2, tm=tm, tn=512)
        h, _ = _ffn_ln(h2, wgu2, wgu2, nf, wd2, g3, b3, tm=tm)
        assert DEPTH == 1
    return h.reshape(b, seq, d)
```

```python
import functools
from typing import NamedTuple

import jax
import jax.numpy as jnp
from jax import lax
from jax.experimental import pallas as pl
from jax.experimental.pallas import tpu as pltpu

N_META = 16
CONV_K = 3
POOL_WINDOWS = (2, 4, 8, 16)
LN_EPS = 1e-5
DEPTH = 1
ALPHA = (2.0 * DEPTH) ** 0.25

BF16 = jnp.bfloat16
F32 = jnp.float32

V7X_MXU_COLS = 256
V7X_VMEM_LIMIT_BYTES = 60 * 1024 * 1024
HALO = 16
LN_ROWS = 8
LN_UNROLL = 4
FFN_DOWN_COLS = 512


def _layer_norm_rows(y, g, b):
    mu = jnp.mean(y, axis=-1, keepdims=True)
    yc = y - mu
    var = jnp.mean(yc * yc, axis=-1, keepdims=True)
    return yc * lax.rsqrt(var + LN_EPS) * g + b


def _residual_layer_norm(o_ref, res_ref, branch_scale, g_ref, b_ref):
    rows = o_ref.shape[0]
    group = min(LN_ROWS * LN_UNROLL, rows)

    def body(r, carry):
        r0 = pl.multiple_of(r * group, group)
        outs = []
        for u in range(0, group, LN_ROWS):
            sub = pl.ds(r0 + u, LN_ROWS)
            y = o_ref[sub, :]
            if res_ref is not None:
                y = ALPHA * res_ref[sub, :] + branch_scale * y
            outs.append(_layer_norm_rows(y, g_ref[...], b_ref[...]))
        for k, u in enumerate(range(0, group, LN_ROWS)):
            o_ref[pl.ds(r0 + u, LN_ROWS), :] = outs[k]
        return carry

    lax.fori_loop(0, rows // group, body, 0)


def _swiglu_chunk(xb, wg, wu):
    gate = jnp.dot(xb, wg, preferred_element_type=F32)
    up = jnp.dot(xb, wu, preferred_element_type=F32)
    return (gate * jax.nn.sigmoid(gate) * up).astype(BF16)


class _CastJob(NamedTuple):
    src: jax.Array
    block: tuple
    start: int
    panel: int = 0

    @property
    def n_blocks(self):
        return (self.src.shape[0] // self.block[0]) * (self.src.shape[1] // self.block[1])

    @property
    def out_shape(self):
        rows, cols = self.src.shape
        shape = (cols // self.panel, rows, self.panel) if self.panel else (rows, cols)
        return jax.ShapeDtypeStruct(shape, BF16)

    def _block_index(self, steps_per_row_tile, i, j):
        col_blocks = self.src.shape[1] // self.block[1]
        s = jnp.clip(i * steps_per_row_tile + j - self.start, 0, self.n_blocks - 1)
        return s // col_blocks, s % col_blocks

    def in_spec(self, steps_per_row_tile):
        return pl.BlockSpec(self.block, functools.partial(self._block_index, steps_per_row_tile))

    def out_spec(self, steps_per_row_tile):
        if not self.panel:
            return self.in_spec(steps_per_row_tile)
        br, bc = self.block

        def index_map(i, j):
            rb, cb = self._block_index(steps_per_row_tile, i, j)
            return (cb, rb, 0)

        return pl.BlockSpec((bc // self.panel, br, self.panel), index_map)


def _cast_block(src_ref, dst_ref):
    if len(dst_ref.shape) == 2:
        dst_ref[...] = src_ref[...].astype(BF16)
    else:
        panel = dst_ref.shape[2]
        for p in range(dst_ref.shape[0]):
            dst_ref[p] = src_ref[:, p * panel:(p + 1) * panel].astype(BF16)


def _ffn_kernel(*refs, cast_ranges):
    n_cast = len(cast_ranges)
    x_ref, wg_ref, wu_ref, wd_ref, g_ref, b_ref = refs[:6]
    cast_in = refs[6:6 + n_cast]
    n_in = len(refs) - (2 + n_cast)
    o_ref = refs[n_in]
    cast_out = refs[n_in + 1:n_in + 1 + n_cast]
    xb_ref = refs[n_in + 1 + n_cast]
    j = pl.program_id(1)
    step = pl.program_id(0) * pl.num_programs(1) + j

    @pl.when(j == 0)
    def _():
        xb_ref[...] = x_ref[...].astype(BF16)
        o_ref[...] = jnp.zeros_like(o_ref)

    for src_ref, dst_ref, (start, stop) in zip(cast_in, cast_out, cast_ranges):
        @pl.when(jnp.logical_and(step >= start, step < stop))
        def _(src_ref=src_ref, dst_ref=dst_ref):
            _cast_block(src_ref, dst_ref)

    act = _swiglu_chunk(xb_ref[...], wg_ref[...], wu_ref[...])
    d = o_ref.shape[1]
    for n0 in range(0, d, FFN_DOWN_COLS):
        cols = slice(n0, n0 + FFN_DOWN_COLS)
        o_ref[:, cols] += jnp.dot(act, wd_ref[:, cols], preferred_element_type=F32)

    @pl.when(j == pl.num_programs(1) - 1)
    def _():
        _residual_layer_norm(o_ref, x_ref, 0.5, g_ref, b_ref)


def _ffn_ln(x, wg, wu, up_block_offset, w_down, ln_g, ln_b, *, tm, out_init=None, cast_jobs=()):
    m, d = x.shape
    f = w_down.shape[0]
    tf = wg.shape[2]
    assert m % tm == 0 and f % tf == 0 and tf == V7X_MXU_COLS
    nf = f // tf
    first_tile = 0 if out_init is None else 1
    row_tiles = m // tm - first_tile
    for job in cast_jobs:
        assert job.start + job.n_blocks <= row_tiles * nf
    row_spec = pl.BlockSpec((tm, d), lambda i, j: (i + first_tile, 0))
    operands = [x, wg, wu, w_down, ln_g, ln_b] + [job.src for job in cast_jobs]
    if out_init is not None:
        operands.append(out_init)
    outs = pl.pallas_call(
        functools.partial(
            _ffn_kernel,
            cast_ranges=tuple((job.start, job.start + job.n_blocks) for job in cast_jobs)),
        grid=(row_tiles, nf),
        in_specs=[
            row_spec,
            pl.BlockSpec((None, d, tf), lambda i, j: (j, 0, 0)),
            pl.BlockSpec((None, d, tf), lambda i, j: (up_block_offset + j, 0, 0)),
            pl.BlockSpec((tf, d), lambda i, j: (j, 0)),
            pl.BlockSpec((1, d), lambda i, j: (0, 0)),
            pl.BlockSpec((1, d), lambda i, j: (0, 0)),
        ] + [job.in_spec(nf) for job in cast_jobs]
        + ([pl.BlockSpec(memory_space=pl.ANY)] if out_init is not None else []),
        out_specs=[row_spec] + [job.out_spec(nf) for job in cast_jobs],
        out_shape=[jax.ShapeDtypeStruct((m, d), F32)] + [job.out_shape for job in cast_jobs],
        input_output_aliases={} if out_init is None else {len(operands) - 1: 0},
        scratch_shapes=[pltpu.VMEM((tm, d), BF16)],
        compiler_params=pltpu.CompilerParams(
            dimension_semantics=("arbitrary", "arbitrary"),
            vmem_limit_bytes=V7X_VMEM_LIMIT_BYTES,
        ),
        name="ffn_ln",
    )(*operands)
    return outs[0], tuple(outs[1:])


def _ffn_head_kernel(x_ref, meta_ref, wg_ref, wu_ref, wd_ref, g_ref, b_ref,
                     o_ref, om_ref, wgb_ref, wub_ref, wdb_ref, xb_ref):
    j = pl.program_id(0)
    n_meta = meta_ref.shape[0]

    @pl.when(j == 0)
    def _():
        xb_ref[0:n_meta, :] = meta_ref[...].astype(BF16)
        xb_ref[n_meta:, :] = x_ref[...].astype(BF16)
        o_ref[...] = jnp.zeros_like(o_ref)
        om_ref[...] = jnp.zeros_like(om_ref)

    wgb_ref[...] = wg_ref[...].astype(BF16)
    wub_ref[...] = wu_ref[...].astype(BF16)
    wdb_ref[...] = wd_ref[...].astype(BF16)
    act = _swiglu_chunk(xb_ref[...], wgb_ref[...], wub_ref[...])
    d = o_ref.shape[1]
    for n0 in range(0, d, FFN_DOWN_COLS):
        cols = slice(n0, n0 + FFN_DOWN_COLS)
        part = jnp.dot(act, wdb_ref[:, cols], preferred_element_type=F32)
        om_ref[:, cols] += part[0:n_meta, :]
        o_ref[:, cols] += part[n_meta:, :]

    @pl.when(j == pl.num_programs(0) - 1)
    def _():
        _residual_layer_norm(om_ref, meta_ref, 0.5, g_ref, b_ref)
        _residual_layer_norm(o_ref, x_ref, 0.5, g_ref, b_ref)


def _ffn_ln_head(x, meta, w_gu, w_down, ln_g, ln_b, *, tm):
    m, d = x.shape
    n_meta = meta.shape[0]
    f = w_down.shape[0]
    tf = V7X_MXU_COLS
    nf = f // tf
    resident = pl.Buffered(1)
    return pl.pallas_call(
        _ffn_head_kernel,
        grid=(nf,),
        in_specs=[
            pl.BlockSpec((tm, d), lambda j: (0, 0), pipeline_mode=resident),
            pl.BlockSpec((n_meta, d), lambda j: (0, 0), pipeline_mode=resident),
            pl.BlockSpec((d, tf), lambda j: (0, j)),
            pl.BlockSpec((d, tf), lambda j: (0, nf + j)),
            pl.BlockSpec((tf, d), lambda j: (j, 0)),
            pl.BlockSpec((1, d), lambda j: (0, 0)),
            pl.BlockSpec((1, d), lambda j: (0, 0)),
        ],
        out_specs=[
            pl.BlockSpec((tm, d), lambda j: (0, 0), pipeline_mode=resident),
            pl.BlockSpec((n_meta, d), lambda j: (0, 0), pipeline_mode=resident),
            pl.BlockSpec((None, d, tf), lambda j: (j, 0, 0)),
            pl.BlockSpec((None, d, tf), lambda j: (j, 0, 0)),
            pl.BlockSpec((tf, d), lambda j: (j, 0)),
        ],
        out_shape=[
            jax.ShapeDtypeStruct((m, d), F32),
            jax.ShapeDtypeStruct((n_meta, d), F32),
            jax.ShapeDtypeStruct((nf, d, tf), BF16),
            jax.ShapeDtypeStruct((nf, d, tf), BF16),
            jax.ShapeDtypeStruct((f, d), BF16),
        ],
        scratch_shapes=[pltpu.VMEM((n_meta + tm, d), BF16)],
        compiler_params=pltpu.CompilerParams(
            dimension_semantics=("arbitrary",),
            vmem_limit_bytes=V7X_VMEM_LIMIT_BYTES,
        ),
        name="ffn_ln_head",
    )(x, meta, w_gu, w_gu, w_down, ln_g, ln_b)


def _assemble_lhs(hb_ref, main_ref, halo_ref, meta_ref, tiles_per_seq):
    first = (pl.program_id(0) % tiles_per_seq) == 0

    @pl.when(first)
    def _():
        hb_ref[0:HALO, :] = meta_ref[...].astype(BF16)

    @pl.when(jnp.logical_not(first))
    def _():
        hb_ref[0:HALO, :] = halo_ref[...].astype(BF16)

    hb_ref[HALO:, :] = main_ref[...].astype(BF16)


def _conv_kernel(main_ref, halo_ref, meta_ref, wb_ref, wc_ref, wx_ref, cw_ref, y_ref, hb_ref,
                 *, tiles_per_seq):
    @pl.when(pl.program_id(1) == 0)
    def _():
        _assemble_lhs(hb_ref, main_ref, halo_ref, meta_ref, tiles_per_seq)

    hb = hb_ref[...]
    gate_b = jnp.dot(hb, wb_ref[...], preferred_element_type=F32)
    gate_c = jnp.dot(hb, wc_ref[...], preferred_element_type=F32)
    x_in = jnp.dot(hb, wx_ref[...], preferred_element_type=F32)
    v = gate_c * x_in
    conv = pltpu.roll(v, 2, 0) * cw_ref[0:1, :]
    conv = conv + pltpu.roll(v, 1, 0) * cw_ref[1:2, :]
    conv = conv + v * cw_ref[2:3, :]
    y_ref[...] = (gate_b * conv)[HALO:, :].astype(BF16)


def _pool_kernel(main_ref, halo_ref, meta_ref, wp_ref, pw_ref, ps_ref, y_ref, hb_ref,
                 *, tiles_per_seq):
    g = pl.program_id(1)

    @pl.when(g == 0)
    def _():
        _assemble_lhs(hb_ref, main_ref, halo_ref, meta_ref, tiles_per_seq)

    hb = hb_ref[...]
    z = jnp.concatenate(
        [jnp.dot(hb, wp_ref[p], preferred_element_type=F32) for p in range(wp_ref.shape[0])], axis=1)
    s = z
    for level in range(len(POOL_WINDOWS)):
        doubled = s + pltpu.roll(s, 1 << level, 0)
        s = doubled if level == 0 else jnp.where(level <= g, doubled, s)
    inv_w = jnp.where(g == 0, 1.0 / POOL_WINDOWS[0],
                      jnp.where(g == 1, 1.0 / POOL_WINDOWS[1],
                                jnp.where(g == 2, 1.0 / POOL_WINDOWS[2], 1.0 / POOL_WINDOWS[3])))
    diff = (s * inv_w.astype(F32) - z)[HALO:, :].astype(BF16)
    y = jnp.dot(diff, pw_ref[...], preferred_element_type=F32) * ps_ref[...]
    y_ref[...] = y.astype(BF16)


def _mixer_inputs(h, *, ts, seq):
    m, d = h.shape
    tiles_per_seq = seq // ts
    halo_blocks_per_tile = ts // HALO
    specs = [
        pl.BlockSpec((ts, d), lambda t, c: (t, 0)),
        pl.BlockSpec((HALO, d), lambda t, c: (jnp.maximum(t * halo_blocks_per_tile - 1, 0), 0)),
        pl.BlockSpec((HALO, d), lambda t, c: (0, 0)),
    ]
    return m // ts, tiles_per_seq, specs


def _conv_mixer(h, meta_h, w_in, conv_w, *, ts, seq):
    d = h.shape[1]
    cc = conv_w.shape[1]
    tc = w_in.shape[2]
    nc = cc // tc
    n_tiles, tiles_per_seq, lhs_specs = _mixer_inputs(h, ts=ts, seq=seq)
    return pl.pallas_call(
        functools.partial(_conv_kernel, tiles_per_seq=tiles_per_seq),
        grid=(n_tiles, nc),
        in_specs=lhs_specs + [
            pl.BlockSpec((None, d, tc), lambda t, c: (c, 0, 0)),
            pl.BlockSpec((None, d, tc), lambda t, c: (nc + c, 0, 0)),
            pl.BlockSpec((None, d, tc), lambda t, c: (2 * nc + c, 0, 0)),
            pl.BlockSpec((CONV_K, tc), lambda t, c: (0, c)),
        ],
        out_specs=pl.BlockSpec((ts, tc), lambda t, c: (t, c)),
        out_shape=jax.ShapeDtypeStruct((h.shape[0], cc), BF16),
        scratch_shapes=[pltpu.VMEM((ts + HALO, d), BF16)],
        compiler_params=pltpu.CompilerParams(
            dimension_semantics=("arbitrary", "arbitrary"),
            vmem_limit_bytes=V7X_VMEM_LIMIT_BYTES,
        ),
        name="mixer_conv",
    )(h, h, meta_h, w_in, w_in, w_in, conv_w)


def _pool_mixer(h, meta_h, w_in, pool_w, pool_scale, *, ts, seq):
    d = h.shape[1]
    n_groups, pg, _ = pool_w.shape
    n_panels, _, tc = w_in.shape
    panels_per_group = pg // tc
    first_pool_block = n_panels // panels_per_group - n_groups
    n_tiles, tiles_per_seq, lhs_specs = _mixer_inputs(h, ts=ts, seq=seq)
    return pl.pallas_call(
        functools.partial(_pool_kernel, tiles_per_seq=tiles_per_seq),
        grid=(n_tiles, n_groups),
        in_specs=lhs_specs + [
            pl.BlockSpec((panels_per_group, d, tc), lambda t, g: (first_pool_block + g, 0, 0)),
            pl.BlockSpec((None, pg, pg), lambda t, g: (g, 0, 0)),
            pl.BlockSpec((1, pg), lambda t, g: (0, g)),
        ],
        out_specs=pl.BlockSpec((ts, pg), lambda t, g: (t, g)),
        out_shape=jax.ShapeDtypeStruct((h.shape[0], n_groups * pg), BF16),
        scratch_shapes=[pltpu.VMEM((ts + HALO, d), BF16)],
        compiler_params=pltpu.CompilerParams(
            dimension_semantics=("arbitrary", "arbitrary"),
            vmem_limit_bytes=V7X_VMEM_LIMIT_BYTES,
        ),
        name="mixer_pool",
    )(h, h, meta_h, w_in, pool_w, pool_scale)


def _out_kernel(yc_ref, yp_ref, wc_ref, wp_ref, h_ref, g_ref, b_ref, o_ref, *, tn):
    n = pl.program_id(1)
    acc = jnp.dot(yc_ref[...], wc_ref[...], preferred_element_type=F32)
    acc = acc + jnp.dot(yp_ref[...], wp_ref[...], preferred_element_type=F32)
    col = pl.multiple_of(n * tn, tn)
    o_ref[:, pl.ds(col, tn)] = ALPHA * h_ref[...] + acc

    @pl.when(n == pl.num_programs(1) - 1)
    def _():
        _residual_layer_norm(o_ref, None, None, g_ref, b_ref)


def _out_proj_ln(yc, yp, w_out, h, ln_g, ln_b, *, tm, tn):
    m, d = h.shape
    cc = yc.shape[1]
    pc = yp.shape[1]
    assert cc == pc and m % tm == 0 and d % tn == 0
    return pl.pallas_call(
        functools.partial(_out_kernel, tn=tn),
        grid=(m // tm, d // tn),
        in_specs=[
            pl.BlockSpec((tm, cc), lambda i, n: (i, 0)),
            pl.BlockSpec((tm, pc), lambda i, n: (i, 0)),
            pl.BlockSpec((cc, tn), lambda i, n: (0, n)),
            pl.BlockSpec((pc, tn), lambda i, n: (1, n)),
            pl.BlockSpec((tm, tn), lambda i, n: (i, n)),
            pl.BlockSpec((1, d), lambda i, n: (0, 0)),
            pl.BlockSpec((1, d), lambda i, n: (0, 0)),
        ],
        out_specs=pl.BlockSpec((tm, d), lambda i, n: (i, 0)),
        out_shape=jax.ShapeDtypeStruct((m, d), F32),
        compiler_params=pltpu.CompilerParams(
            dimension_semantics=("arbitrary", "arbitrary"),
            vmem_limit_bytes=V7X_VMEM_LIMIT_BYTES,
        ),
        name="mixer_out_ln",
    )(yc, yp, w_out, w_out, h, ln_g, ln_b)


def kernel(x, meta_tokens, ffn1_w_gu, ffn1_w_down, ln1_g, ln1_b, w_in, conv_w, pool_w, pool_scale,
           w_out, ln2_g, ln2_b, ffn2_w_gu, ffn2_w_down, ln3_g, ln3_b):
    b, seq, d = x.shape
    assert meta_tokens.shape == (N_META, d) and N_META == HALO
    assert ffn1_w_gu.shape[0] == DEPTH
    tm = 512

    h = x.reshape(b * seq, d)
    meta = meta_tokens.astype(x.dtype)
    for i in range(DEPTH):
        g1, b1 = ln1_g[i][None].astype(F32), ln1_b[i][None].astype(F32)
        g2, b2 = ln2_g[i][None].astype(F32), ln2_b[i][None].astype(F32)
        g3, b3 = ln3_g[i][None].astype(F32), ln3_b[i][None].astype(F32)
        n_groups, pg, _ = pool_w[i].shape
        nf = ffn1_w_down[i].shape[0] // V7X_MXU_COLS

        h1_head, meta1, wg1, wu1, wd1 = _ffn_ln_head(
            h, meta, ffn1_w_gu[i], ffn1_w_down[i], g1, b1, tm=tm)
        jobs = (
            _CastJob(ffn2_w_gu[i], (16, ffn2_w_gu[i].shape[1] // 2), 0, panel=V7X_MXU_COLS),
            _CastJob(ffn2_w_down[i], (32, d), 0),
            _CastJob(w_out[i], (32, d), ffn2_w_down[i].shape[0] // 32),
            _CastJob(w_in[i], (16, w_in[i].shape[1]), 0, panel=V7X_MXU_COLS),
            _CastJob(pool_w[i].reshape(n_groups * pg, pg), (128, pg), w_in[i].shape[0] // 16),
        )
        h1, (wgu2, wd2, w_out_b, w_in_b, pool_w_b) = _ffn_ln(
            h, wg1, wu1, 0, wd1, g1, b1, tm=tm, out_init=h1_head, cast_jobs=jobs)
        pool_w_b = pool_w_b.reshape(n_groups, pg, pg)

        y_conv = _conv_mixer(h1, meta1, w_in_b, conv_w[i], ts=tm, seq=seq)
        y_pool = _pool_mixer(h1, meta1, w_in_b, pool_w_b, pool_scale[i][None], ts=tm, seq=seq)
        h2 = _out_proj_ln(y_conv, y_pool, w_out_b, h1, g2, b2, tm=tm, tn=512)
        h, _ = _ffn_ln(h2, wgu2, wgu2, nf, wd2, g3, b3, tm=tm)
        assert DEPTH == 1
    return h.reshape(b, seq, d)
```

```python
import functools
from typing import NamedTuple

import jax
import jax.numpy as jnp
from jax import lax
from jax.experimental import pallas as pl
from jax.experimental.pallas import tpu as pltpu

N_META = 16
CONV_K = 3
POOL_WINDOWS = (2, 4, 8, 16)
LN_EPS = 1e-5
DEPTH = 1
ALPHA = (2.0 * DEPTH) ** 0.25

BF16 = jnp.bfloat16
F32 = jnp.float32

V7X_MXU_COLS = 256
V7X_VMEM_LIMIT_BYTES = 60 * 1024 * 1024
HALO = 16
LN_ROWS = 8
LN_UNROLL = 4
FFN_DOWN_COLS = 512
X_SLAB_COLS = 128
OUT_COPIES_IN_FLIGHT = 4


def _layer_norm_rows(y, g, b):
    mu = jnp.mean(y, axis=-1, keepdims=True)
    yc = y - mu
    var = jnp.mean(yc * yc, axis=-1, keepdims=True)
    return yc * lax.rsqrt(var + LN_EPS) * g + b


def _ln_group_rows(rows):
    return min(LN_ROWS * LN_UNROLL, rows)


def _layer_norm_group(o_ref, r0, res_ref, branch_scale, g_ref, b_ref):
    group = _ln_group_rows(o_ref.shape[0])
    outs = []
    for u in range(0, group, LN_ROWS):
        sub = pl.ds(r0 + u, LN_ROWS)
        y = o_ref[sub, :]
        if branch_scale is not None:
            y = branch_scale * y
        if res_ref is not None:
            y = ALPHA * res_ref[sub, :] + y
        outs.append(_layer_norm_rows(y, g_ref[...], b_ref[...]))
    for k, u in enumerate(range(0, group, LN_ROWS)):
        o_ref[pl.ds(r0 + u, LN_ROWS), :] = outs[k]


def _residual_layer_norm(o_ref, res_ref, branch_scale, g_ref, b_ref):
    rows = o_ref.shape[0]
    group = _ln_group_rows(rows)

    def body(r, carry):
        _layer_norm_group(o_ref, pl.multiple_of(r * group, group), res_ref, branch_scale, g_ref, b_ref)
        return carry

    lax.fori_loop(0, rows // group, body, 0)


def _swiglu_chunk(xb, wg, wu):
    gate = jnp.dot(xb, wg, preferred_element_type=F32)
    up = jnp.dot(xb, wu, preferred_element_type=F32)
    return (gate * jax.nn.sigmoid(gate) * up).astype(BF16)


class _CastJob(NamedTuple):
    src: jax.Array
    block: tuple
    start: int
    panel: int = 0

    @property
    def n_blocks(self):
        return (self.src.shape[0] // self.block[0]) * (self.src.shape[1] // self.block[1])

    @property
    def out_shape(self):
        rows, cols = self.src.shape
        shape = (cols // self.panel, rows, self.panel) if self.panel else (rows, cols)
        return jax.ShapeDtypeStruct(shape, BF16)

    def _block_index(self, steps_per_row_tile, i, j):
        col_blocks = self.src.shape[1] // self.block[1]
        s = jnp.clip(i * steps_per_row_tile + j - self.start, 0, self.n_blocks - 1)
        return s // col_blocks, s % col_blocks

    def in_spec(self, steps_per_row_tile):
        return pl.BlockSpec(self.block, functools.partial(self._block_index, steps_per_row_tile))

    def out_spec(self, steps_per_row_tile):
        if not self.panel:
            return self.in_spec(steps_per_row_tile)
        br, bc = self.block

        def index_map(i, j):
            rb, cb = self._block_index(steps_per_row_tile, i, j)
            return (cb, rb, 0)

        return pl.BlockSpec((bc // self.panel, br, self.panel), index_map)


def _cast_block(src_ref, dst_ref):
    if len(dst_ref.shape) == 2:
        dst_ref[...] = src_ref[...].astype(BF16)
    else:
        panel = dst_ref.shape[2]
        for p in range(dst_ref.shape[0]):
            dst_ref[p] = src_ref[:, p * panel:(p + 1) * panel].astype(BF16)


def _ffn_kernel(*refs, n_tiles, tm, cast_ranges):
    n_cast = len(cast_ranges)
    xres_ref, xnext_ref, wg_ref, wu_ref, wd_ref, g_ref, b_ref = refs[:7]
    cast_in = refs[7:7 + n_cast]
    n_in = len(refs) - (5 + n_cast)
    tail_hbm = refs[7 + n_cast] if n_in > 7 + n_cast else None
    out_hbm = refs[n_in]
    cast_out = refs[n_in + 1:n_in + 1 + n_cast]
    acc_ref, xb_ref, sem, tail_sem = refs[n_in + 1 + n_cast:]
    i = pl.program_id(0)
    j = pl.program_id(1)
    nf = pl.num_programs(1)
    d = acc_ref.shape[1]
    n_slabs = d // X_SLAB_COLS
    slab = pl.ds(pl.multiple_of(jnp.minimum(j, n_slabs - 1) * X_SLAB_COLS, X_SLAB_COLS), X_SLAB_COLS)
    step = (i - 1) * nf + j

    if tail_hbm is not None:
        tail_copy = pltpu.make_async_copy(
            tail_hbm, out_hbm.at[pl.ds(n_tiles * tm, tail_hbm.shape[0])], tail_sem)

        @pl.when(jnp.logical_and(i == 0, j == 0))
        def _():
            tail_copy.start()

        @pl.when(jnp.logical_and(i == n_tiles, j == nf - 1))
        def _():
            tail_copy.wait()

    @pl.when(jnp.logical_and(i < n_tiles, j < n_slabs))
    def _():
        xb_ref.at[i % 2][:, slab] = xnext_ref[...].astype(BF16)

    for src_ref, dst_ref, (start, stop) in zip(cast_in, cast_out, cast_ranges):
        @pl.when(jnp.logical_and(step >= start, step < stop))
        def _(src_ref=src_ref, dst_ref=dst_ref):
            _cast_block(src_ref, dst_ref)

    @pl.when(i > 0)
    def _():
        @pl.when(j == 0)
        def _():
            acc_ref[...] = jnp.zeros_like(acc_ref)

        @pl.when(j < n_slabs)
        def _():
            acc_ref[:, slab] += (ALPHA / 0.5) * xres_ref[...]

        act = _swiglu_chunk(xb_ref[(i - 1) % 2], wg_ref[...], wu_ref[...])
        for n0 in range(0, d, FFN_DOWN_COLS):
            cols = slice(n0, n0 + FFN_DOWN_COLS)
            acc_ref[:, cols] += jnp.dot(act, wd_ref[:, cols], preferred_element_type=F32)

        @pl.when(j == nf - 1)
        def _():
            _layer_norm_and_write_back(acc_ref, out_hbm, sem, (i - 1) * tm, g_ref, b_ref)


def _layer_norm_and_write_back(acc_ref, out_hbm, sem, row0, g_ref, b_ref):
    rows = acc_ref.shape[0]
    group = _ln_group_rows(rows)
    n_groups = rows // group
    in_flight = min(OUT_COPIES_IN_FLIGHT, n_groups)

    def group_copy(k):
        r0 = pl.multiple_of(k * group, group)
        return pltpu.make_async_copy(
            acc_ref.at[pl.ds(r0, group)], out_hbm.at[pl.ds(row0 + r0, group)], sem.at[k % in_flight])

    def body(k, carry):
        @pl.when(k >= in_flight)
        def _():
            group_copy(k - in_flight).wait()

        _layer_norm_group(acc_ref, pl.multiple_of(k * group, group), None, 0.5, g_ref, b_ref)
        group_copy(k).start()
        return carry

    lax.fori_loop(0, n_groups, body, 0)
    for k in range(n_groups - in_flight, n_groups):
        group_copy(k).wait()


def _ffn_ln(x, wg, wu, up_block_offset, w_down, ln_g, ln_b, *, tm, tail=None, cast_jobs=()):
    m, d = x.shape
    f = w_down.shape[0]
    tf = wg.shape[2]
    n_tiles = (m - (0 if tail is None else tail.shape[0])) // tm
    assert n_tiles * tm + (0 if tail is None else tail.shape[0]) == m
    assert f % tf == 0 and tf == V7X_MXU_COLS and tm % _ln_group_rows(tm) == 0
    nf = f // tf
    n_slabs = d // X_SLAB_COLS
    assert n_slabs <= nf
    for job in cast_jobs:
        assert job.start + job.n_blocks <= n_tiles * nf

    def weight_chunk(i, j):
        return jnp.where(i == 0, 0, j)

    def slab_col(j):
        return jnp.minimum(j, n_slabs - 1)

    shifted_jobs = [job._replace(start=job.start + nf) for job in cast_jobs]
    operands = [x, x, wg, wu, w_down, ln_g, ln_b] + [job.src for job in cast_jobs]
    if tail is not None:
        operands.append(tail)
    outs = pl.pallas_call(
        functools.partial(
            _ffn_kernel, n_tiles=n_tiles, tm=tm,
            cast_ranges=tuple((job.start, job.start + job.n_blocks) for job in cast_jobs)),
        grid=(n_tiles + 1, nf),
        in_specs=[
            pl.BlockSpec((tm, X_SLAB_COLS), lambda i, j: (jnp.maximum(i - 1, 0), slab_col(j))),
            pl.BlockSpec((tm, X_SLAB_COLS), lambda i, j: (jnp.minimum(i, n_tiles - 1), slab_col(j))),
            pl.BlockSpec((None, d, tf), lambda i, j: (weight_chunk(i, j), 0, 0)),
            pl.BlockSpec((None, d, tf), lambda i, j: (up_block_offset + weight_chunk(i, j), 0, 0)),
            pl.BlockSpec((tf, d), lambda i, j: (weight_chunk(i, j), 0)),
            pl.BlockSpec((1, d), lambda i, j: (0, 0)),
            pl.BlockSpec((1, d), lambda i, j: (0, 0)),
        ] + [job.in_spec(nf) for job in shifted_jobs]
        + ([pl.BlockSpec(memory_space=pl.ANY)] if tail is not None else []),
        out_specs=[pl.BlockSpec(memory_space=pl.ANY)] + [job.out_spec(nf) for job in shifted_jobs],
        out_shape=[jax.ShapeDtypeStruct((m, d), F32)] + [job.out_shape for job in cast_jobs],
        scratch_shapes=[
            pltpu.VMEM((tm, d), F32),
            pltpu.VMEM((2, tm, d), BF16),
            pltpu.SemaphoreType.DMA((OUT_COPIES_IN_FLIGHT,)),
            pltpu.SemaphoreType.DMA(()),
        ],
        compiler_params=pltpu.CompilerParams(
            dimension_semantics=("arbitrary", "arbitrary"),
            vmem_limit_bytes=V7X_VMEM_LIMIT_BYTES,
        ),
        name="ffn_ln",
    )(*operands)
    return outs[0], tuple(outs[1:])


def _ffn_head_kernel(x_ref, meta_ref, wg_ref, wu_ref, wd_ref, g_ref, b_ref,
                     o_ref, om_ref, wgb_ref, wub_ref, wdb_ref, xb_ref):
    j = pl.program_id(0)
    n_meta = meta_ref.shape[0]

    @pl.when(j == 0)
    def _():
        xb_ref[0:n_meta, :] = meta_ref[...].astype(BF16)
        xb_ref[n_meta:, :] = x_ref[...].astype(BF16)
        o_ref[...] = jnp.zeros_like(o_ref)
        om_ref[...] = jnp.zeros_like(om_ref)

    wgb_ref[...] = wg_ref[...].astype(BF16)
    wub_ref[...] = wu_ref[...].astype(BF16)
    wdb_ref[...] = wd_ref[...].astype(BF16)
    act = _swiglu_chunk(xb_ref[...], wgb_ref[...], wub_ref[...])
    d = o_ref.shape[1]
    for n0 in range(0, d, FFN_DOWN_COLS):
        cols = slice(n0, n0 + FFN_DOWN_COLS)
        part = jnp.dot(act, wdb_ref[:, cols], preferred_element_type=F32)
        om_ref[:, cols] += part[0:n_meta, :]
        o_ref[:, cols] += part[n_meta:, :]

    @pl.when(j == pl.num_programs(0) - 1)
    def _():
        _residual_layer_norm(om_ref, meta_ref, 0.5, g_ref, b_ref)
        _residual_layer_norm(o_ref, x_ref, 0.5, g_ref, b_ref)


def _ffn_ln_head(x, meta, w_gu, w_down, ln_g, ln_b, *, tm):
    m, d = x.shape
    n_meta = meta.shape[0]
    f = w_down.shape[0]
    tf = V7X_MXU_COLS
    nf = f // tf
    assert m % tm == 0
    last = m // tm - 1
    resident = pl.Buffered(1)
    return pl.pallas_call(
        _ffn_head_kernel,
        grid=(nf,),
        in_specs=[
            pl.BlockSpec((tm, d), lambda j: (last, 0), pipeline_mode=resident),
            pl.BlockSpec((n_meta, d), lambda j: (0, 0), pipeline_mode=resident),
            pl.BlockSpec((d, tf), lambda j: (0, j)),
            pl.BlockSpec((d, tf), lambda j: (0, nf + j)),
            pl.BlockSpec((tf, d), lambda j: (j, 0)),
            pl.BlockSpec((1, d), lambda j: (0, 0)),
            pl.BlockSpec((1, d), lambda j: (0, 0)),
        ],
        out_specs=[
            pl.BlockSpec((tm, d), lambda j: (0, 0), pipeline_mode=resident),
            pl.BlockSpec((n_meta, d), lambda j: (0, 0), pipeline_mode=resident),
            pl.BlockSpec((None, d, tf), lambda j: (j, 0, 0)),
            pl.BlockSpec((None, d, tf), lambda j: (j, 0, 0)),
            pl.BlockSpec((tf, d), lambda j: (j, 0)),
        ],
        out_shape=[
            jax.ShapeDtypeStruct((tm, d), F32),
            jax.ShapeDtypeStruct((n_meta, d), F32),
            jax.ShapeDtypeStruct((nf, d, tf), BF16),
            jax.ShapeDtypeStruct((nf, d, tf), BF16),
            jax.ShapeDtypeStruct((f, d), BF16),
        ],
        scratch_shapes=[pltpu.VMEM((n_meta + tm, d), BF16)],
        compiler_params=pltpu.CompilerParams(
            dimension_semantics=("arbitrary",),
            vmem_limit_bytes=V7X_VMEM_LIMIT_BYTES,
        ),
        name="ffn_ln_head",
    )(x, meta, w_gu, w_gu, w_down, ln_g, ln_b)


def _assemble_lhs(hb_ref, main_ref, halo_ref, meta_ref, tiles_per_seq):
    first = (pl.program_id(0) % tiles_per_seq) == 0

    @pl.when(first)
    def _():
        hb_ref[0:HALO, :] = meta_ref[...].astype(BF16)

    @pl.when(jnp.logical_not(first))
    def _():
        hb_ref[0:HALO, :] = halo_ref[...].astype(BF16)

    hb_ref[HALO:, :] = main_ref[...].astype(BF16)


def _conv_kernel(main_ref, halo_ref, meta_ref, wb_ref, wc_ref, wx_ref, cw_ref, y_ref, hb_ref,
                 *, tiles_per_seq):
    @pl.when(pl.program_id(1) == 0)
    def _():
        _assemble_lhs(hb_ref, main_ref, halo_ref, meta_ref, tiles_per_seq)

    hb = hb_ref[...]
    gate_b = jnp.dot(hb, wb_ref[...], preferred_element_type=F32)
    gate_c = jnp.dot(hb, wc_ref[...], preferred_element_type=F32)
    x_in = jnp.dot(hb, wx_ref[...], preferred_element_type=F32)
    v = gate_c * x_in
    conv = pltpu.roll(v, 2, 0) * cw_ref[0:1, :]
    conv = conv + pltpu.roll(v, 1, 0) * cw_ref[1:2, :]
    conv = conv + v * cw_ref[2:3, :]
    y_ref[...] = (gate_b * conv)[HALO:, :].astype(BF16)


def _pool_kernel(main_ref, halo_ref, meta_ref, wp_ref, pw_ref, ps_ref, y_ref, hb_ref,
                 *, tiles_per_seq):
    g = pl.program_id(1)

    @pl.when(g == 0)
    def _():
        _assemble_lhs(hb_ref, main_ref, halo_ref, meta_ref, tiles_per_seq)

    hb = hb_ref[...]
    z = jnp.concatenate(
        [jnp.dot(hb, wp_ref[p], preferred_element_type=F32) for p in range(wp_ref.shape[0])], axis=1)
    s = z
    for level in range(len(POOL_WINDOWS)):
        doubled = s + pltpu.roll(s, 1 << level, 0)
        s = doubled if level == 0 else jnp.where(level <= g, doubled, s)
    inv_w = jnp.where(g == 0, 1.0 / POOL_WINDOWS[0],
                      jnp.where(g == 1, 1.0 / POOL_WINDOWS[1],
                                jnp.where(g == 2, 1.0 / POOL_WINDOWS[2], 1.0 / POOL_WINDOWS[3])))
    diff = (s * inv_w.astype(F32) - z)[HALO:, :].astype(BF16)
    y = jnp.dot(diff, pw_ref[...], preferred_element_type=F32) * ps_ref[...]
    y_ref[...] = y.astype(BF16)


def _mixer_inputs(h, *, ts, seq):
    m, d = h.shape
    tiles_per_seq = seq // ts
    halo_blocks_per_tile = ts // HALO
    specs = [
        pl.BlockSpec((ts, d), lambda t, c: (t, 0)),
        pl.BlockSpec((HALO, d), lambda t, c: (jnp.maximum(t * halo_blocks_per_tile - 1, 0), 0)),
        pl.BlockSpec((HALO, d), lambda t, c: (0, 0)),
    ]
    return m // ts, tiles_per_seq, specs


def _conv_mixer(h, meta_h, w_in, conv_w, *, ts, seq):
    d = h.shape[1]
    cc = conv_w.shape[1]
    tc = w_in.shape[2]
    nc = cc // tc
    n_tiles, tiles_per_seq, lhs_specs = _mixer_inputs(h, ts=ts, seq=seq)
    return pl.pallas_call(
        functools.partial(_conv_kernel, tiles_per_seq=tiles_per_seq),
        grid=(n_tiles, nc),
        in_specs=lhs_specs + [
            pl.BlockSpec((None, d, tc), lambda t, c: (c, 0, 0)),
            pl.BlockSpec((None, d, tc), lambda t, c: (nc + c, 0, 0)),
            pl.BlockSpec((None, d, tc), lambda t, c: (2 * nc + c, 0, 0)),
            pl.BlockSpec((CONV_K, tc), lambda t, c: (0, c)),
        ],
        out_specs=pl.BlockSpec((ts, tc), lambda t, c: (t, c)),
        out_shape=jax.ShapeDtypeStruct((h.shape[0], cc), BF16),
        scratch_shapes=[pltpu.VMEM((ts + HALO, d), BF16)],
        compiler_params=pltpu.CompilerParams(
            dimension_semantics=("arbitrary", "arbitrary"),
            vmem_limit_bytes=V7X_VMEM_LIMIT_BYTES,
        ),
        name="mixer_conv",
    )(h, h, meta_h, w_in, w_in, w_in, conv_w)


def _pool_mixer(h, meta_h, w_in, pool_w, pool_scale, *, ts, seq):
    d = h.shape[1]
    n_groups, pg, _ = pool_w.shape
    n_panels, _, tc = w_in.shape
    panels_per_group = pg // tc
    first_pool_block = n_panels // panels_per_group - n_groups
    n_tiles, tiles_per_seq, lhs_specs = _mixer_inputs(h, ts=ts, seq=seq)
    return pl.pallas_call(
        functools.partial(_pool_kernel, tiles_per_seq=tiles_per_seq),
        grid=(n_tiles, n_groups),
        in_specs=lhs_specs + [
            pl.BlockSpec((panels_per_group, d, tc), lambda t, g: (first_pool_block + g, 0, 0)),
            pl.BlockSpec((None, pg, pg), lambda t, g: (g, 0, 0)),
            pl.BlockSpec((1, pg), lambda t, g: (0, g)),
        ],
        out_specs=pl.BlockSpec((ts, pg), lambda t, g: (t, g)),
        out_shape=jax.ShapeDtypeStruct((h.shape[0], n_groups * pg), BF16),
        scratch_shapes=[pltpu.VMEM((ts + HALO, d), BF16)],
        compiler_params=pltpu.CompilerParams(
            dimension_semantics=("arbitrary", "arbitrary"),
            vmem_limit_bytes=V7X_VMEM_LIMIT_BYTES,
        ),
        name="mixer_pool",
    )(h, h, meta_h, w_in, pool_w, pool_scale)


def _out_kernel(yc_ref, yp_ref, wc_ref, wp_ref, h_ref, g_ref, b_ref, o_ref, *, tn):
    n = pl.program_id(1)
    acc = jnp.dot(yc_ref[...], wc_ref[...], preferred_element_type=F32)
    acc = acc + jnp.dot(yp_ref[...], wp_ref[...], preferred_element_type=F32)
    col = pl.multiple_of(n * tn, tn)
    o_ref[:, pl.ds(col, tn)] = ALPHA * h_ref[...] + acc

    @pl.when(n == pl.num_programs(1) - 1)
    def _():
        _residual_layer_norm(o_ref, None, None, g_ref, b_ref)


def _out_proj_ln(yc, yp, w_out, h, ln_g, ln_b, *, tm, tn):
    m, d = h.shape
    cc = yc.shape[1]
    pc = yp.shape[1]
    assert cc == pc and m % tm == 0 and d % tn == 0
    return pl.pallas_call(
        functools.partial(_out_kernel, tn=tn),
        grid=(m // tm, d // tn),
        in_specs=[
            pl.BlockSpec((tm, cc), lambda i, n: (i, 0)),
            pl.BlockSpec((tm, pc), lambda i, n: (i, 0)),
            pl.BlockSpec((cc, tn), lambda i, n: (0, n)),
            pl.BlockSpec((pc, tn), lambda i, n: (1, n)),
            pl.BlockSpec((tm, tn), lambda i, n: (i, n)),
            pl.BlockSpec((1, d), lambda i, n: (0, 0)),
            pl.BlockSpec((1, d), lambda i, n: (0, 0)),
        ],
        out_specs=pl.BlockSpec((tm, d), lambda i, n: (i, 0)),
        out_shape=jax.ShapeDtypeStruct((m, d), F32),
        compiler_params=pltpu.CompilerParams(
            dimension_semantics=("arbitrary", "arbitrary"),
            vmem_limit_bytes=V7X_VMEM_LIMIT_BYTES,
        ),
        name="mixer_out_ln",
    )(yc, yp, w_out, w_out, h, ln_g, ln_b)


def kernel(x, meta_tokens, ffn1_w_gu, ffn1_w_down, ln1_g, ln1_b, w_in, conv_w, pool_w, pool_scale,
           w_out, ln2_g, ln2_b, ffn2_w_gu, ffn2_w_down, ln3_g, ln3_b):
    b, seq, d = x.shape
    assert meta_tokens.shape == (N_META, d) and N_META == HALO
    assert ffn1_w_gu.shape[0] == DEPTH
    tm = 512

    h = x.reshape(b * seq, d)
    meta = meta_tokens.astype(x.dtype)
    for i in range(DEPTH):
        g1, b1 = ln1_g[i][None].astype(F32), ln1_b[i][None].astype(F32)
        g2, b2 = ln2_g[i][None].astype(F32), ln2_b[i][None].astype(F32)
        g3, b3 = ln3_g[i][None].astype(F32), ln3_b[i][None].astype(F32)
        n_groups, pg, _ = pool_w[i].shape
        nf = ffn1_w_down[i].shape[0] // V7X_MXU_COLS

        h1_head, meta1, wg1, wu1, wd1 = _ffn_ln_head(
            h, meta, ffn1_w_gu[i], ffn1_w_down[i], g1, b1, tm=tm)
        jobs = (
            _CastJob(ffn2_w_gu[i], (32, ffn2_w_gu[i].shape[1] // 2), 0, panel=V7X_MXU_COLS),
            _CastJob(ffn2_w_down[i], (64, d), 0),
            _CastJob(w_out[i], (64, d), ffn2_w_down[i].shape[0] // 64),
            _CastJob(w_in[i], (32, w_in[i].shape[1]), 0, panel=V7X_MXU_COLS),
            _CastJob(pool_w[i].reshape(n_groups * pg, pg), (128, pg), w_in[i].shape[0] // 32),
        )
        h1, (wgu2, wd2, w_out_b, w_in_b, pool_w_b) = _ffn_ln(
            h, wg1, wu1, 0, wd1, g1, b1, tm=768, tail=h1_head, cast_jobs=jobs)
        pool_w_b = pool_w_b.reshape(n_groups, pg, pg)

        y_conv = _conv_mixer(h1, meta1, w_in_b, conv_w[i], ts=tm, seq=seq)
        y_pool = _pool_mixer(h1, meta1, w_in_b, pool_w_b, pool_scale[i][None], ts=tm, seq=seq)
        h2 = _out_proj_ln(y_conv, y_pool, w_out_b, h1, g2, b2, tm=tm, tn=512)
        h, _ = _ffn_ln(h2, wgu2, wgu2, nf, wd2, g3, b3, tm=1024)
        assert DEPTH == 1
    return h.reshape(b, seq, d)
```

```python
import functools
from typing import NamedTuple

import jax
import jax.numpy as jnp
from jax import lax
from jax.experimental import pallas as pl
from jax.experimental.pallas import tpu as pltpu

N_META = 16
CONV_K = 3
POOL_WINDOWS = (2, 4, 8, 16)
LN_EPS = 1e-5
DEPTH = 1
ALPHA = (2.0 * DEPTH) ** 0.25

BF16 = jnp.bfloat16
F32 = jnp.float32

V7X_MXU_COLS = 256
V7X_VMEM_LIMIT_BYTES = 60 * 1024 * 1024
HALO = 16
LN_ROWS = 8
LN_UNROLL = 4
FFN_DOWN_COLS = 512
X_SLAB_COLS = 128
MIXER_SLAB_COLS = 256
OUT_COPIES_IN_FLIGHT = 4


def _layer_norm_rows(y, g, b):
    mu = jnp.mean(y, axis=-1, keepdims=True)
    yc = y - mu
    var = jnp.mean(yc * yc, axis=-1, keepdims=True)
    return yc * lax.rsqrt(var + LN_EPS) * g + b


def _ln_group_rows(rows):
    return min(LN_ROWS * LN_UNROLL, rows)


def _layer_norm_group(o_ref, r0, res_ref, branch_scale, g_ref, b_ref):
    group = _ln_group_rows(o_ref.shape[0])
    outs = []
    for u in range(0, group, LN_ROWS):
        sub = pl.ds(r0 + u, LN_ROWS)
        y = o_ref[sub, :]
        if branch_scale is not None:
            y = branch_scale * y
        if res_ref is not None:
            y = ALPHA * res_ref[sub, :] + y
        outs.append(_layer_norm_rows(y, g_ref[...], b_ref[...]))
    for k, u in enumerate(range(0, group, LN_ROWS)):
        o_ref[pl.ds(r0 + u, LN_ROWS), :] = outs[k]


def _residual_layer_norm(o_ref, res_ref, branch_scale, g_ref, b_ref):
    rows = o_ref.shape[0]
    group = _ln_group_rows(rows)

    def body(r, carry):
        _layer_norm_group(o_ref, pl.multiple_of(r * group, group), res_ref, branch_scale, g_ref, b_ref)
        return carry

    lax.fori_loop(0, rows // group, body, 0)


def _swiglu_chunk(xb, wg, wu):
    gate = jnp.dot(xb, wg, preferred_element_type=F32)
    up = jnp.dot(xb, wu, preferred_element_type=F32)
    return (gate * jax.nn.sigmoid(gate) * up).astype(BF16)


class _CastJob(NamedTuple):
    src: jax.Array
    block: tuple
    start: int
    panel: int = 0

    @property
    def n_blocks(self):
        return (self.src.shape[0] // self.block[0]) * (self.src.shape[1] // self.block[1])

    @property
    def out_shape(self):
        rows, cols = self.src.shape
        shape = (cols // self.panel, rows, self.panel) if self.panel else (rows, cols)
        return jax.ShapeDtypeStruct(shape, BF16)

    def _block_index(self, steps_per_row_tile, i, j):
        col_blocks = self.src.shape[1] // self.block[1]
        s = jnp.clip(i * steps_per_row_tile + j - self.start, 0, self.n_blocks - 1)
        return s // col_blocks, s % col_blocks

    def in_spec(self, steps_per_row_tile):
        return pl.BlockSpec(self.block, functools.partial(self._block_index, steps_per_row_tile))

    def out_spec(self, steps_per_row_tile):
        if not self.panel:
            return self.in_spec(steps_per_row_tile)
        br, bc = self.block

        def index_map(i, j):
            rb, cb = self._block_index(steps_per_row_tile, i, j)
            return (cb, rb, 0)

        return pl.BlockSpec((bc // self.panel, br, self.panel), index_map)


def _cast_block(src_ref, dst_ref):
    if len(dst_ref.shape) == 2:
        dst_ref[...] = src_ref[...].astype(BF16)
    else:
        panel = dst_ref.shape[2]
        for p in range(dst_ref.shape[0]):
            dst_ref[p] = src_ref[:, p * panel:(p + 1) * panel].astype(BF16)


def _ffn_kernel(*refs, n_tiles, tm, n_cast):
    xres_ref, xnext_ref, wg_ref, wu_ref, wd_ref, g_ref, b_ref = refs[:7]
    cast_in = refs[7:7 + n_cast]
    n_in = len(refs) - (5 + n_cast)
    tail_hbm = refs[7 + n_cast] if n_in > 7 + n_cast else None
    out_hbm = refs[n_in]
    cast_out = refs[n_in + 1:n_in + 1 + n_cast]
    acc_ref, xb_ref, sem, tail_sem = refs[n_in + 1 + n_cast:]
    i = pl.program_id(0)
    j = pl.program_id(1)
    nf = pl.num_programs(1)
    d = acc_ref.shape[1]
    n_slabs = d // X_SLAB_COLS
    slab = pl.ds(pl.multiple_of(jnp.minimum(j, n_slabs - 1) * X_SLAB_COLS, X_SLAB_COLS), X_SLAB_COLS)

    if tail_hbm is not None:
        tail_copy = pltpu.make_async_copy(
            tail_hbm, out_hbm.at[pl.ds(n_tiles * tm, tail_hbm.shape[0])], tail_sem)

        @pl.when(jnp.logical_and(i == 0, j == 0))
        def _():
            tail_copy.start()

        @pl.when(jnp.logical_and(i == n_tiles, j == nf - 1))
        def _():
            tail_copy.wait()

    def round_next_slab():
        xb_ref.at[i % 2][:, slab] = xnext_ref[...].astype(BF16)

    @pl.when(i == 0)
    def _():
        round_next_slab()

    @pl.when(i > 0)
    def _():
        @pl.when(j == 0)
        def _():
            acc_ref[...] = jnp.zeros_like(acc_ref)

        act = _swiglu_chunk(xb_ref[(i - 1) % 2], wg_ref[...], wu_ref[...])
        round_next_slab()
        residual_scale = jnp.where(j < n_slabs, ALPHA / 0.5, 0.0).astype(F32)
        acc_ref[:, slab] += residual_scale * xres_ref[...]
        fillers = list(zip(cast_in, cast_out))
        for n0 in range(0, d, FFN_DOWN_COLS):
            cols = slice(n0, n0 + FFN_DOWN_COLS)
            acc_ref[:, cols] += jnp.dot(act, wd_ref[:, cols], preferred_element_type=F32)
            if fillers:
                _cast_block(*fillers.pop(0))
        assert not fillers

        @pl.when(j == nf - 1)
        def _():
            _layer_norm_and_write_back(acc_ref, out_hbm, sem, (i - 1) * tm, 0.5, g_ref, b_ref)


def _layer_norm_and_write_back(acc_ref, out_hbm, sem, row0, branch_scale, g_ref, b_ref):
    rows = acc_ref.shape[0]
    group = _ln_group_rows(rows)
    n_groups = rows // group
    in_flight = min(OUT_COPIES_IN_FLIGHT, n_groups)

    def group_copy(k):
        r0 = pl.multiple_of(k * group, group)
        return pltpu.make_async_copy(
            acc_ref.at[pl.ds(r0, group)], out_hbm.at[pl.ds(row0 + r0, group)], sem.at[k % in_flight])

    def body(k, carry):
        @pl.when(k >= in_flight)
        def _():
            group_copy(k - in_flight).wait()

        _layer_norm_group(acc_ref, pl.multiple_of(k * group, group), None, branch_scale, g_ref, b_ref)
        group_copy(k).start()
        return carry

    lax.fori_loop(0, n_groups, body, 0)
    for k in range(n_groups - in_flight, n_groups):
        group_copy(k).wait()


def _ffn_ln(x, wg, wu, up_block_offset, w_down, ln_g, ln_b, *, tm, tail=None, cast_jobs=()):
    m, d = x.shape
    f = w_down.shape[0]
    tf = wg.shape[2]
    n_tiles = (m - (0 if tail is None else tail.shape[0])) // tm
    assert n_tiles * tm + (0 if tail is None else tail.shape[0]) == m
    assert f % tf == 0 and tf == V7X_MXU_COLS and tm % _ln_group_rows(tm) == 0
    nf = f // tf
    n_slabs = d // X_SLAB_COLS
    assert n_slabs <= nf
    for job in cast_jobs:
        assert job.start + job.n_blocks <= n_tiles * nf

    def weight_chunk(i, j):
        return jnp.where(i == 0, 0, j)

    def slab_col(j):
        return jnp.minimum(j, n_slabs - 1)

    shifted_jobs = [job._replace(start=job.start + nf) for job in cast_jobs]
    operands = [x, x, wg, wu, w_down, ln_g, ln_b] + [job.src for job in cast_jobs]
    if tail is not None:
        operands.append(tail)
    outs = pl.pallas_call(
        functools.partial(_ffn_kernel, n_tiles=n_tiles, tm=tm, n_cast=len(cast_jobs)),
        grid=(n_tiles + 1, nf),
        in_specs=[
            pl.BlockSpec((tm, X_SLAB_COLS), lambda i, j: (jnp.maximum(i - 1, 0), slab_col(j))),
            pl.BlockSpec((tm, X_SLAB_COLS), lambda i, j: (jnp.minimum(i, n_tiles - 1), slab_col(j))),
            pl.BlockSpec((None, d, tf), lambda i, j: (weight_chunk(i, j), 0, 0)),
            pl.BlockSpec((None, d, tf), lambda i, j: (up_block_offset + weight_chunk(i, j), 0, 0)),
            pl.BlockSpec((tf, d), lambda i, j: (weight_chunk(i, j), 0)),
            pl.BlockSpec((1, d), lambda i, j: (0, 0)),
            pl.BlockSpec((1, d), lambda i, j: (0, 0)),
        ] + [job.in_spec(nf) for job in shifted_jobs]
        + ([pl.BlockSpec(memory_space=pl.ANY)] if tail is not None else []),
        out_specs=[pl.BlockSpec(memory_space=pl.ANY)] + [job.out_spec(nf) for job in shifted_jobs],
        out_shape=[jax.ShapeDtypeStruct((m, d), F32)] + [job.out_shape for job in cast_jobs],
        scratch_shapes=[
            pltpu.VMEM((tm, d), F32),
            pltpu.VMEM((2, tm, d), BF16),
            pltpu.SemaphoreType.DMA((OUT_COPIES_IN_FLIGHT,)),
            pltpu.SemaphoreType.DMA(()),
        ],
        compiler_params=pltpu.CompilerParams(
            dimension_semantics=("arbitrary", "arbitrary"),
            vmem_limit_bytes=V7X_VMEM_LIMIT_BYTES,
        ),
        name="ffn_ln",
    )(*operands)
    return outs[0], tuple(outs[1:])


def _ffn_head_kernel(x_ref, meta_ref, wg_ref, wu_ref, wd_ref, g_ref, b_ref,
                     o_ref, om_ref, wgb_ref, wub_ref, wdb_ref, xb_ref):
    j = pl.program_id(0)
    n_meta = meta_ref.shape[0]

    @pl.when(j == 0)
    def _():
        xb_ref[0:n_meta, :] = meta_ref[...].astype(BF16)
        xb_ref[n_meta:, :] = x_ref[...].astype(BF16)
        o_ref[...] = jnp.zeros_like(o_ref)
        om_ref[...] = jnp.zeros_like(om_ref)

    wgb_ref[...] = wg_ref[...].astype(BF16)
    wub_ref[...] = wu_ref[...].astype(BF16)
    wdb_ref[...] = wd_ref[...].astype(BF16)
    act = _swiglu_chunk(xb_ref[...], wgb_ref[...], wub_ref[...])
    d = o_ref.shape[1]
    for n0 in range(0, d, FFN_DOWN_COLS):
        cols = slice(n0, n0 + FFN_DOWN_COLS)
        part = jnp.dot(act, wdb_ref[:, cols], preferred_element_type=F32)
        om_ref[:, cols] += part[0:n_meta, :]
        o_ref[:, cols] += part[n_meta:, :]

    @pl.when(j == pl.num_programs(0) - 1)
    def _():
        _residual_layer_norm(om_ref, meta_ref, 0.5, g_ref, b_ref)
        _residual_layer_norm(o_ref, x_ref, 0.5, g_ref, b_ref)


def _ffn_ln_head(x, meta, w_gu, w_down, ln_g, ln_b, *, tm):
    m, d = x.shape
    n_meta = meta.shape[0]
    f = w_down.shape[0]
    tf = V7X_MXU_COLS
    nf = f // tf
    assert m % tm == 0
    last = m // tm - 1
    resident = pl.Buffered(1)
    return pl.pallas_call(
        _ffn_head_kernel,
        grid=(nf,),
        in_specs=[
            pl.BlockSpec((tm, d), lambda j: (last, 0), pipeline_mode=resident),
            pl.BlockSpec((n_meta, d), lambda j: (0, 0), pipeline_mode=resident),
            pl.BlockSpec((d, tf), lambda j: (0, j)),
            pl.BlockSpec((d, tf), lambda j: (0, nf + j)),
            pl.BlockSpec((tf, d), lambda j: (j, 0)),
            pl.BlockSpec((1, d), lambda j: (0, 0)),
            pl.BlockSpec((1, d), lambda j: (0, 0)),
        ],
        out_specs=[
            pl.BlockSpec((tm, d), lambda j: (0, 0), pipeline_mode=resident),
            pl.BlockSpec((n_meta, d), lambda j: (0, 0), pipeline_mode=resident),
            pl.BlockSpec((None, d, tf), lambda j: (j, 0, 0)),
            pl.BlockSpec((None, d, tf), lambda j: (j, 0, 0)),
            pl.BlockSpec((tf, d), lambda j: (j, 0)),
        ],
        out_shape=[
            jax.ShapeDtypeStruct((tm, d), F32),
            jax.ShapeDtypeStruct((n_meta, d), F32),
            jax.ShapeDtypeStruct((nf, d, tf), BF16),
            jax.ShapeDtypeStruct((nf, d, tf), BF16),
            jax.ShapeDtypeStruct((f, d), BF16),
        ],
        scratch_shapes=[pltpu.VMEM((n_meta + tm, d), BF16)],
        compiler_params=pltpu.CompilerParams(
            dimension_semantics=("arbitrary",),
            vmem_limit_bytes=V7X_VMEM_LIMIT_BYTES,
        ),
        name="ffn_ln_head",
    )(x, meta, w_gu, w_gu, w_down, ln_g, ln_b)


def _mixer_kernel(hres_ref, hnext_ref, halo_ref, meta_ref, wa_ref, wb_ref, wc_ref, wo_ref, pw_ref,
                  cw_ref, ps_ref, g_ref, b_ref, out_hbm, acc_ref, xb_ref, yp_ref, sem,
                  *, ts, tiles_per_seq, n_conv):
    i = pl.program_id(0)
    c = pl.program_id(1)
    d = acc_ref.shape[1]
    tc = wa_ref.shape[1]
    n_slabs = d // MIXER_SLAB_COLS
    slab = pl.ds(
        pl.multiple_of(jnp.minimum(c, n_slabs - 1) * MIXER_SLAB_COLS, MIXER_SLAB_COLS), MIXER_SLAB_COLS)

    def round_next_slab():
        first = (i % tiles_per_seq) == 0
        nxt = xb_ref.at[i % 2]
        nxt[0:HALO, slab] = jnp.where(first, meta_ref[...], halo_ref[...]).astype(BF16)
        nxt[HALO:, slab] = hnext_ref[...].astype(BF16)

    def stream_and_accumulate(y):
        round_next_slab()
        residual_scale = jnp.where(c < n_slabs, ALPHA, 0.0).astype(F32)
        acc_ref[:, slab] += residual_scale * hres_ref[...]
        for n0 in range(0, d, FFN_DOWN_COLS):
            cols = slice(n0, n0 + FFN_DOWN_COLS)
            acc_ref[:, cols] += jnp.dot(y, wo_ref[:, cols], preferred_element_type=F32)

    @pl.when(i == 0)
    def _():
        round_next_slab()

    @pl.when(i > 0)
    def _():
        @pl.when(c == 0)
        def _():
            acc_ref[...] = jnp.zeros_like(acc_ref)

        pool_step = c - n_conv
        group = pool_step // 2

        @pl.when(c < n_conv)
        def _():
            half = HALO + (ts // 2)
            hb_ref = xb_ref.at[(i - 1) % 2]
            tops = [jnp.dot(hb_ref[0:half, :], w[...], preferred_element_type=F32)
                    for w in (wa_ref, wb_ref, wc_ref)]
            bots = [jnp.dot(hb_ref[half:, :], w[...], preferred_element_type=F32)
                    for w in (wa_ref, wb_ref, wc_ref)]
            gate_b, gate_c, x_in = [jnp.concatenate(tb, axis=0) for tb in zip(tops, bots)]
            v = gate_c * x_in
            conv = pltpu.roll(v, 2, 0) * cw_ref[0:1, :]
            conv = conv + pltpu.roll(v, 1, 0) * cw_ref[1:2, :]
            conv = conv + v * cw_ref[2:3, :]
            stream_and_accumulate((gate_b * conv)[HALO:, :].astype(BF16))

        @pl.when(jnp.logical_and(pool_step >= 0, pool_step % 2 == 0))
        def _():
            hb = xb_ref[(i - 1) % 2]
            z = jnp.concatenate([jnp.dot(hb, wa_ref[...], preferred_element_type=F32),
                                 jnp.dot(hb, wb_ref[...], preferred_element_type=F32)], axis=1)
            s = z
            for level in range(len(POOL_WINDOWS)):
                doubled = s + pltpu.roll(s, 1 << level, 0)
                s = doubled if level == 0 else jnp.where(level <= group, doubled, s)
            inv_w = jnp.where(group == 0, 1.0 / POOL_WINDOWS[0],
                              jnp.where(group == 1, 1.0 / POOL_WINDOWS[1],
                                        jnp.where(group == 2, 1.0 / POOL_WINDOWS[2],
                                                  1.0 / POOL_WINDOWS[3])))
            diff = (s * inv_w.astype(F32) - z)[HALO:, :].astype(BF16)
            y = (jnp.dot(diff, pw_ref[...], preferred_element_type=F32) * ps_ref[...]).astype(BF16)
            yp_ref[...] = y
            stream_and_accumulate(y[:, 0:tc])

        @pl.when(jnp.logical_and(pool_step >= 0, pool_step % 2 == 1))
        def _():
            stream_and_accumulate(yp_ref[:, tc:])

        @pl.when(c == pl.num_programs(1) - 1)
        def _():
            _layer_norm_and_write_back(acc_ref, out_hbm, sem, (i - 1) * ts, None, g_ref, b_ref)


def _mixer_ln(h, meta_h, w_in, conv_w, pool_w, pool_scale, w_out, ln_g, ln_b, *, ts, seq):
    m, d = h.shape
    cc = conv_w.shape[1]
    n_groups, pg, _ = pool_w.shape
    n_panels, _, tc = w_in.shape
    n_conv = cc // tc
    assert pg == 2 * tc and n_panels == 3 * n_conv + 2 * n_groups and w_out.shape[0] == cc + n_groups * pg
    n_steps = n_conv + 2 * n_groups
    n_tiles = m // ts
    tiles_per_seq = seq // ts
    n_slabs = d // MIXER_SLAB_COLS
    assert m % ts == 0 and seq % ts == 0 and n_slabs <= n_steps and ts % _ln_group_rows(ts) == 0

    def step(i, c):
        return jnp.where(i == 0, 0, c)

    def group(i, c):
        return jnp.maximum(step(i, c) - n_conv, 0) // 2

    def slab_col(c):
        return jnp.minimum(c, n_slabs - 1)

    def next_tile(i):
        return jnp.minimum(i, n_tiles - 1)

    def pool_or(i, c, conv_panel, pool_panel):
        return jnp.where(step(i, c) < n_conv, conv_panel, pool_panel)

    return pl.pallas_call(
        functools.partial(_mixer_kernel, ts=ts, tiles_per_seq=tiles_per_seq, n_conv=n_conv),
        grid=(n_tiles + 1, n_steps),
        in_specs=[
            pl.BlockSpec((ts, MIXER_SLAB_COLS), lambda i, c: (jnp.maximum(i - 1, 0), slab_col(c))),
            pl.BlockSpec((ts, MIXER_SLAB_COLS), lambda i, c: (next_tile(i), slab_col(c))),
            pl.BlockSpec((HALO, MIXER_SLAB_COLS),
                         lambda i, c: (jnp.maximum(next_tile(i) * (ts // HALO) - 1, 0), slab_col(c))),
            pl.BlockSpec((HALO, MIXER_SLAB_COLS), lambda i, c: (0, slab_col(c))),
            pl.BlockSpec((None, d, tc), lambda i, c: (
                pool_or(i, c, step(i, c), 3 * n_conv + 2 * group(i, c)), 0, 0)),
            pl.BlockSpec((None, d, tc), lambda i, c: (
                pool_or(i, c, n_conv + step(i, c), 3 * n_conv + 2 * group(i, c) + 1), 0, 0)),
            pl.BlockSpec((None, d, tc), lambda i, c: (
                2 * n_conv + jnp.minimum(step(i, c), n_conv - 1), 0, 0)),
            pl.BlockSpec((tc, d), lambda i, c: (step(i, c), 0)),
            pl.BlockSpec((None, pg, pg), lambda i, c: (group(i, c), 0, 0)),
            pl.BlockSpec((CONV_K, tc), lambda i, c: (0, jnp.minimum(step(i, c), n_conv - 1))),
            pl.BlockSpec((1, pg), lambda i, c: (0, group(i, c))),
            pl.BlockSpec((1, d), lambda i, c: (0, 0)),
            pl.BlockSpec((1, d), lambda i, c: (0, 0)),
        ],
        out_specs=pl.BlockSpec(memory_space=pl.ANY),
        out_shape=jax.ShapeDtypeStruct((m, d), F32),
        scratch_shapes=[
            pltpu.VMEM((ts, d), F32),
            pltpu.VMEM((2, HALO + ts, d), BF16),
            pltpu.VMEM((ts, pg), BF16),
            pltpu.SemaphoreType.DMA((OUT_COPIES_IN_FLIGHT,)),
        ],
        compiler_params=pltpu.CompilerParams(
            dimension_semantics=("arbitrary", "arbitrary"),
            vmem_limit_bytes=V7X_VMEM_LIMIT_BYTES,
        ),
        name="mixer_ln",
    )(h, h, h, meta_h, w_in, w_in, w_in, w_out, pool_w, conv_w, pool_scale, ln_g, ln_b)


def kernel(x, meta_tokens, ffn1_w_gu, ffn1_w_down, ln1_g, ln1_b, w_in, conv_w, pool_w, pool_scale,
           w_out, ln2_g, ln2_b, ffn2_w_gu, ffn2_w_down, ln3_g, ln3_b):
    b, seq, d = x.shape
    assert meta_tokens.shape == (N_META, d) and N_META == HALO
    assert ffn1_w_gu.shape[0] == DEPTH
    tm = 512

    h = x.reshape(b * seq, d)
    meta = meta_tokens.astype(x.dtype)
    for i in range(DEPTH):
        g1, b1 = ln1_g[i][None].astype(F32), ln1_b[i][None].astype(F32)
        g2, b2 = ln2_g[i][None].astype(F32), ln2_b[i][None].astype(F32)
        g3, b3 = ln3_g[i][None].astype(F32), ln3_b[i][None].astype(F32)
        n_groups, pg, _ = pool_w[i].shape
        nf = ffn1_w_down[i].shape[0] // V7X_MXU_COLS

        h1_head, meta1, wg1, wu1, wd1 = _ffn_ln_head(
            h, meta, ffn1_w_gu[i], ffn1_w_down[i], g1, b1, tm=tm)
        jobs = (
            _CastJob(ffn2_w_gu[i], (32, ffn2_w_gu[i].shape[1] // 2), 0, panel=V7X_MXU_COLS),
            _CastJob(ffn2_w_down[i], (64, d), 0),
            _CastJob(w_out[i], (64, d), ffn2_w_down[i].shape[0] // 64),
            _CastJob(w_in[i], (32, w_in[i].shape[1]), 0, panel=V7X_MXU_COLS),
            _CastJob(pool_w[i].reshape(n_groups * pg, pg), (128, pg), w_in[i].shape[0] // 32),
        )
        h1, (wgu2, wd2, w_out_b, w_in_b, pool_w_b) = _ffn_ln(
            h, wg1, wu1, 0, wd1, g1, b1, tm=768, tail=h1_head, cast_jobs=jobs)
        pool_w_b = pool_w_b.reshape(n_groups, pg, pg)

        h2 = _mixer_ln(h1, meta1, w_in_b, conv_w[i], pool_w_b, pool_scale[i][None], w_out_b, g2, b2,
                       ts=1024, seq=seq)
        h, _ = _ffn_ln(h2, wgu2, wgu2, nf, wd2, g3, b3, tm=1024)
        assert DEPTH == 1
    return h.reshape(b, seq, d)
```

```python
import functools
from typing import NamedTuple

import jax
import jax.numpy as jnp
from jax import lax
from jax.experimental import pallas as pl
from jax.experimental.pallas import tpu as pltpu

N_META = 16
CONV_K = 3
POOL_WINDOWS = (2, 4, 8, 16)
LN_EPS = 1e-5
DEPTH = 1
ALPHA = (2.0 * DEPTH) ** 0.25

BF16 = jnp.bfloat16
F32 = jnp.float32

V7X_MXU_COLS = 256
V7X_VMEM_LIMIT_BYTES = 60 * 1024 * 1024
HALO = 16
LN_ROWS = 8
LN_UNROLL = 4
FFN_DOWN_COLS = 512
X_SLAB_COLS = 128
MIXER_SLAB_COLS = 256
OUT_COPIES_IN_FLIGHT = 4


def _layer_norm_rows(y, g, b):
    mu = jnp.mean(y, axis=-1, keepdims=True)
    yc = y - mu
    var = jnp.mean(yc * yc, axis=-1, keepdims=True)
    return yc * lax.rsqrt(var + LN_EPS) * g + b


def _ln_group_rows(rows):
    return min(LN_ROWS * LN_UNROLL, rows)


def _layer_norm_group(o_ref, r0, res_ref, branch_scale, g_ref, b_ref):
    group = _ln_group_rows(o_ref.shape[0])
    outs = []
    for u in range(0, group, LN_ROWS):
        sub = pl.ds(r0 + u, LN_ROWS)
        y = o_ref[sub, :]
        if branch_scale is not None:
            y = branch_scale * y
        if res_ref is not None:
            y = ALPHA * res_ref[sub, :] + y
        outs.append(_layer_norm_rows(y, g_ref[...], b_ref[...]))
    for k, u in enumerate(range(0, group, LN_ROWS)):
        o_ref[pl.ds(r0 + u, LN_ROWS), :] = outs[k]


def _residual_layer_norm(o_ref, res_ref, branch_scale, g_ref, b_ref):
    rows = o_ref.shape[0]
    group = _ln_group_rows(rows)

    def body(r, carry):
        _layer_norm_group(o_ref, pl.multiple_of(r * group, group), res_ref, branch_scale, g_ref, b_ref)
        return carry

    lax.fori_loop(0, rows // group, body, 0)


def _swiglu_chunk(xb, wg, wu):
    gate = jnp.dot(xb, wg, preferred_element_type=F32)
    up = jnp.dot(xb, wu, preferred_element_type=F32)
    return (gate * jax.nn.sigmoid(gate) * up).astype(BF16)


class _CastJob(NamedTuple):
    src: jax.Array
    block: tuple
    start: int
    panel: int = 0

    @property
    def n_blocks(self):
        return (self.src.shape[0] // self.block[0]) * (self.src.shape[1] // self.block[1])

    @property
    def out_shape(self):
        rows, cols = self.src.shape
        shape = (cols // self.panel, rows, self.panel) if self.panel else (rows, cols)
        return jax.ShapeDtypeStruct(shape, BF16)

    def _block_index(self, steps_per_row_tile, i, j):
        col_blocks = self.src.shape[1] // self.block[1]
        s = jnp.clip(i * steps_per_row_tile + j - self.start, 0, self.n_blocks - 1)
        return s // col_blocks, s % col_blocks

    def in_spec(self, steps_per_row_tile):
        return pl.BlockSpec(self.block, functools.partial(self._block_index, steps_per_row_tile))

    def out_spec(self, steps_per_row_tile):
        if not self.panel:
            return self.in_spec(steps_per_row_tile)
        br, bc = self.block

        def index_map(i, j):
            rb, cb = self._block_index(steps_per_row_tile, i, j)
            return (cb, rb, 0)

        return pl.BlockSpec((bc // self.panel, br, self.panel), index_map)


def _cast_block(src_ref, dst_ref):
    if len(dst_ref.shape) == 2:
        dst_ref[...] = src_ref[...].astype(BF16)
    else:
        panel = dst_ref.shape[2]
        for p in range(dst_ref.shape[0]):
            dst_ref[p] = src_ref[:, p * panel:(p + 1) * panel].astype(BF16)


def _ffn_kernel(*refs, n_tiles, tm, cast_ranges):
    n_cast = len(cast_ranges)
    xres_ref, xnext_ref, wg_ref, wu_ref, wd_ref, g_ref, b_ref = refs[:7]
    cast_in = refs[7:7 + n_cast]
    n_in = len(refs) - (5 + n_cast)
    tail_hbm = refs[7 + n_cast] if n_in > 7 + n_cast else None
    out_hbm = refs[n_in]
    cast_out = refs[n_in + 1:n_in + 1 + n_cast]
    acc_ref, xb_ref, sem, tail_sem = refs[n_in + 1 + n_cast:]
    i = pl.program_id(0)
    j = pl.program_id(1)
    nf = pl.num_programs(1)
    d = acc_ref.shape[1]
    n_slabs = d // X_SLAB_COLS
    slab = pl.ds(pl.multiple_of(jnp.minimum(j, n_slabs - 1) * X_SLAB_COLS, X_SLAB_COLS), X_SLAB_COLS)

    if tail_hbm is not None:
        tail_copy = pltpu.make_async_copy(
            tail_hbm, out_hbm.at[pl.ds(n_tiles * tm, tail_hbm.shape[0])], tail_sem)

        @pl.when(jnp.logical_and(i == 0, j == 0))
        def _():
            tail_copy.start()

        @pl.when(jnp.logical_and(i == n_tiles, j == nf - 1))
        def _():
            tail_copy.wait()

    @pl.when(jnp.logical_and(i < n_tiles, j < n_slabs))
    def _():
        xb_ref.at[i % 2][:, slab] = xnext_ref[...].astype(BF16)

    step = (i - 1) * nf + j
    for src_ref, dst_ref, (start, stop) in zip(cast_in, cast_out, cast_ranges):
        @pl.when(jnp.logical_and(step >= start, step < stop))
        def _(src_ref=src_ref, dst_ref=dst_ref):
            _cast_block(src_ref, dst_ref)

    @pl.when(i > 0)
    def _():
        @pl.when(j == 0)
        def _():
            acc_ref[...] = jnp.zeros_like(acc_ref)

        @pl.when(j < n_slabs)
        def _():
            acc_ref[:, slab] += (ALPHA / 0.5) * xres_ref[...]

        act = _swiglu_chunk(xb_ref[(i - 1) % 2], wg_ref[...], wu_ref[...])
        for n0 in range(0, d, FFN_DOWN_COLS):
            cols = slice(n0, n0 + FFN_DOWN_COLS)
            acc_ref[:, cols] += jnp.dot(act, wd_ref[:, cols], preferred_element_type=F32)

        @pl.when(j == nf - 1)
        def _():
            _layer_norm_and_write_back(acc_ref, out_hbm, sem, (i - 1) * tm, 0.5, g_ref, b_ref)


def _layer_norm_and_write_back(acc_ref, out_hbm, sem, row0, branch_scale, g_ref, b_ref):
    rows = acc_ref.shape[0]
    group = _ln_group_rows(rows)
    n_groups = rows // group
    in_flight = min(OUT_COPIES_IN_FLIGHT, n_groups)

    def group_copy(k):
        r0 = pl.multiple_of(k * group, group)
        return pltpu.make_async_copy(
            acc_ref.at[pl.ds(r0, group)], out_hbm.at[pl.ds(row0 + r0, group)], sem.at[k % in_flight])

    def body(k, carry):
        @pl.when(k >= in_flight)
        def _():
            group_copy(k - in_flight).wait()

        _layer_norm_group(acc_ref, pl.multiple_of(k * group, group), None, branch_scale, g_ref, b_ref)
        group_copy(k).start()
        return carry

    lax.fori_loop(0, n_groups, body, 0)
    for k in range(n_groups - in_flight, n_groups):
        group_copy(k).wait()


def _ffn_ln(x, wg, wu, up_block_offset, w_down, ln_g, ln_b, *, tm, tail=None, cast_jobs=()):
    m, d = x.shape
    f = w_down.shape[0]
    tf = wg.shape[2]
    n_tiles = (m - (0 if tail is None else tail.shape[0])) // tm
    assert n_tiles * tm + (0 if tail is None else tail.shape[0]) == m
    assert f % tf == 0 and tf == V7X_MXU_COLS and tm % _ln_group_rows(tm) == 0
    nf = f // tf
    n_slabs = d // X_SLAB_COLS
    assert n_slabs <= nf
    for job in cast_jobs:
        assert job.start + job.n_blocks <= n_tiles * nf

    def weight_chunk(i, j):
        return jnp.where(i == 0, 0, j)

    def slab_col(j):
        return jnp.minimum(j, n_slabs - 1)

    shifted_jobs = [job._replace(start=job.start + nf) for job in cast_jobs]
    operands = [x, x, wg, wu, w_down, ln_g, ln_b] + [job.src for job in cast_jobs]
    if tail is not None:
        operands.append(tail)
    outs = pl.pallas_call(
        functools.partial(
            _ffn_kernel, n_tiles=n_tiles, tm=tm,
            cast_ranges=tuple((job.start, job.start + job.n_blocks) for job in cast_jobs)),
        grid=(n_tiles + 1, nf),
        in_specs=[
            pl.BlockSpec((tm, X_SLAB_COLS), lambda i, j: (jnp.maximum(i - 1, 0), slab_col(j))),
            pl.BlockSpec((tm, X_SLAB_COLS), lambda i, j: (jnp.minimum(i, n_tiles - 1), slab_col(j))),
            pl.BlockSpec((None, d, tf), lambda i, j: (weight_chunk(i, j), 0, 0)),
            pl.BlockSpec((None, d, tf), lambda i, j: (up_block_offset + weight_chunk(i, j), 0, 0)),
            pl.BlockSpec((tf, d), lambda i, j: (weight_chunk(i, j), 0)),
            pl.BlockSpec((1, d), lambda i, j: (0, 0)),
            pl.BlockSpec((1, d), lambda i, j: (0, 0)),
        ] + [job.in_spec(nf) for job in shifted_jobs]
        + ([pl.BlockSpec(memory_space=pl.ANY)] if tail is not None else []),
        out_specs=[pl.BlockSpec(memory_space=pl.ANY)] + [job.out_spec(nf) for job in shifted_jobs],
        out_shape=[jax.ShapeDtypeStruct((m, d), F32)] + [job.out_shape for job in cast_jobs],
        scratch_shapes=[
            pltpu.VMEM((tm, d), F32),
            pltpu.VMEM((2, tm, d), BF16),
            pltpu.SemaphoreType.DMA((OUT_COPIES_IN_FLIGHT,)),
            pltpu.SemaphoreType.DMA(()),
        ],
        compiler_params=pltpu.CompilerParams(
            dimension_semantics=("arbitrary", "arbitrary"),
            vmem_limit_bytes=V7X_VMEM_LIMIT_BYTES,
        ),
        name="ffn_ln",
    )(*operands)
    return outs[0], tuple(outs[1:])


def _ffn_head_kernel(x_ref, meta_ref, wg_ref, wu_ref, wd_ref, g_ref, b_ref,
                     o_ref, om_ref, wgb_ref, wub_ref, wdb_ref, xb_ref):
    j = pl.program_id(0)
    n_meta = meta_ref.shape[0]

    @pl.when(j == 0)
    def _():
        xb_ref[0:n_meta, :] = meta_ref[...].astype(BF16)
        xb_ref[n_meta:, :] = x_ref[...].astype(BF16)
        o_ref[...] = jnp.zeros_like(o_ref)
        om_ref[...] = jnp.zeros_like(om_ref)

    wgb_ref[...] = wg_ref[...].astype(BF16)
    wub_ref[...] = wu_ref[...].astype(BF16)
    wdb_ref[...] = wd_ref[...].astype(BF16)
    act = _swiglu_chunk(xb_ref[...], wgb_ref[...], wub_ref[...])
    d = o_ref.shape[1]
    for n0 in range(0, d, FFN_DOWN_COLS):
        cols = slice(n0, n0 + FFN_DOWN_COLS)
        part = jnp.dot(act, wdb_ref[:, cols], preferred_element_type=F32)
        om_ref[:, cols] += part[0:n_meta, :]
        o_ref[:, cols] += part[n_meta:, :]

    @pl.when(j == pl.num_programs(0) - 1)
    def _():
        _residual_layer_norm(om_ref, meta_ref, 0.5, g_ref, b_ref)
        _residual_layer_norm(o_ref, x_ref, 0.5, g_ref, b_ref)


def _ffn_ln_head(x, meta, w_gu, w_down, ln_g, ln_b, *, tm):
    m, d = x.shape
    n_meta = meta.shape[0]
    f = w_down.shape[0]
    tf = V7X_MXU_COLS
    nf = f // tf
    assert m % tm == 0
    last = m // tm - 1
    resident = pl.Buffered(1)
    return pl.pallas_call(
        _ffn_head_kernel,
        grid=(nf,),
        in_specs=[
            pl.BlockSpec((tm, d), lambda j: (last, 0), pipeline_mode=resident),
            pl.BlockSpec((n_meta, d), lambda j: (0, 0), pipeline_mode=resident),
            pl.BlockSpec((d, tf), lambda j: (0, j)),
            pl.BlockSpec((d, tf), lambda j: (0, nf + j)),
            pl.BlockSpec((tf, d), lambda j: (j, 0)),
            pl.BlockSpec((1, d), lambda j: (0, 0)),
            pl.BlockSpec((1, d), lambda j: (0, 0)),
        ],
        out_specs=[
            pl.BlockSpec((tm, d), lambda j: (0, 0), pipeline_mode=resident),
            pl.BlockSpec((n_meta, d), lambda j: (0, 0), pipeline_mode=resident),
            pl.BlockSpec((None, d, tf), lambda j: (j, 0, 0)),
            pl.BlockSpec((None, d, tf), lambda j: (j, 0, 0)),
            pl.BlockSpec((tf, d), lambda j: (j, 0)),
        ],
        out_shape=[
            jax.ShapeDtypeStruct((tm, d), F32),
            jax.ShapeDtypeStruct((n_meta, d), F32),
            jax.ShapeDtypeStruct((nf, d, tf), BF16),
            jax.ShapeDtypeStruct((nf, d, tf), BF16),
            jax.ShapeDtypeStruct((f, d), BF16),
        ],
        scratch_shapes=[pltpu.VMEM((n_meta + tm, d), BF16)],
        compiler_params=pltpu.CompilerParams(
            dimension_semantics=("arbitrary",),
            vmem_limit_bytes=V7X_VMEM_LIMIT_BYTES,
        ),
        name="ffn_ln_head",
    )(x, meta, w_gu, w_gu, w_down, ln_g, ln_b)


def _mixer_kernel(hres_ref, hnext_ref, halo_ref, meta_ref, wa_ref, wb_ref, wc_ref, wo_ref, pw_ref,
                  cw_ref, ps_ref, g_ref, b_ref, out_hbm, acc_ref, xb_ref, yp_ref, sem,
                  *, n_tiles, ts, tiles_per_seq, n_conv):
    i = pl.program_id(0)
    c = pl.program_id(1)
    d = acc_ref.shape[1]
    tc = wa_ref.shape[1]
    n_slabs = d // MIXER_SLAB_COLS
    slab = pl.ds(
        pl.multiple_of(jnp.minimum(c, n_slabs - 1) * MIXER_SLAB_COLS, MIXER_SLAB_COLS), MIXER_SLAB_COLS)

    def round_next_slab():
        first = (i % tiles_per_seq) == 0
        nxt = xb_ref.at[i % 2]
        nxt[0:HALO, slab] = jnp.where(first, meta_ref[...], halo_ref[...]).astype(BF16)
        nxt[HALO:, slab] = hnext_ref[...].astype(BF16)

    def accumulate(y):
        for n0 in range(0, d, FFN_DOWN_COLS):
            cols = slice(n0, n0 + FFN_DOWN_COLS)
            acc_ref[:, cols] += jnp.dot(y, wo_ref[:, cols], preferred_element_type=F32)

    @pl.when(jnp.logical_and(i < n_tiles, c < n_slabs))
    def _():
        round_next_slab()

    @pl.when(i > 0)
    def _():
        @pl.when(c == 0)
        def _():
            acc_ref[...] = jnp.zeros_like(acc_ref)

        @pl.when(c < n_slabs)
        def _():
            acc_ref[:, slab] += ALPHA * hres_ref[...]

        pool_step = c - n_conv
        group = pool_step // 2

        @pl.when(c < n_conv)
        def _():
            half = HALO + (ts // 2)
            hb_ref = xb_ref.at[(i - 1) % 2]
            tops = [jnp.dot(hb_ref[0:half, :], w[...], preferred_element_type=F32)
                    for w in (wa_ref, wb_ref, wc_ref)]
            bots = [jnp.dot(hb_ref[half:, :], w[...], preferred_element_type=F32)
                    for w in (wa_ref, wb_ref, wc_ref)]
            gate_b, gate_c, x_in = [jnp.concatenate(tb, axis=0) for tb in zip(tops, bots)]
            v = gate_c * x_in
            conv = pltpu.roll(v, 2, 0) * cw_ref[0:1, :]
            conv = conv + pltpu.roll(v, 1, 0) * cw_ref[1:2, :]
            conv = conv + v * cw_ref[2:3, :]
            accumulate((gate_b * conv)[HALO:, :].astype(BF16))

        @pl.when(jnp.logical_and(pool_step >= 0, pool_step % 2 == 0))
        def _():
            hb = xb_ref[(i - 1) % 2]
            z = jnp.concatenate([jnp.dot(hb, wa_ref[...], preferred_element_type=F32),
                                 jnp.dot(hb, wb_ref[...], preferred_element_type=F32)], axis=1)
            s = z
            for level in range(len(POOL_WINDOWS)):
                doubled = s + pltpu.roll(s, 1 << level, 0)
                s = doubled if level == 0 else jnp.where(level <= group, doubled, s)
            inv_w = jnp.where(group == 0, 1.0 / POOL_WINDOWS[0],
                              jnp.where(group == 1, 1.0 / POOL_WINDOWS[1],
                                        jnp.where(group == 2, 1.0 / POOL_WINDOWS[2],
                                                  1.0 / POOL_WINDOWS[3])))
            diff = (s * inv_w.astype(F32) - z)[HALO:, :].astype(BF16)
            y = (jnp.dot(diff, pw_ref[...], preferred_element_type=F32) * ps_ref[...]).astype(BF16)
            yp_ref[...] = y
            accumulate(y[:, 0:tc])

        @pl.when(jnp.logical_and(pool_step >= 0, pool_step % 2 == 1))
        def _():
            accumulate(yp_ref[:, tc:])

        @pl.when(c == pl.num_programs(1) - 1)
        def _():
            _layer_norm_and_write_back(acc_ref, out_hbm, sem, (i - 1) * ts, None, g_ref, b_ref)


def _mixer_ln(h, meta_h, w_in, conv_w, pool_w, pool_scale, w_out, ln_g, ln_b, *, ts, seq):
    m, d = h.shape
    cc = conv_w.shape[1]
    n_groups, pg, _ = pool_w.shape
    n_panels, _, tc = w_in.shape
    n_conv = cc // tc
    assert pg == 2 * tc and n_panels == 3 * n_conv + 2 * n_groups and w_out.shape[0] == cc + n_groups * pg
    n_steps = n_conv + 2 * n_groups
    n_tiles = m // ts
    tiles_per_seq = seq // ts
    n_slabs = d // MIXER_SLAB_COLS
    assert m % ts == 0 and seq % ts == 0 and n_slabs <= n_steps and ts % _ln_group_rows(ts) == 0

    def step(i, c):
        return jnp.where(i == 0, 0, c)

    def group(i, c):
        return jnp.maximum(step(i, c) - n_conv, 0) // 2

    def slab_col(c):
        return jnp.minimum(c, n_slabs - 1)

    def next_tile(i):
        return jnp.minimum(i, n_tiles - 1)

    def pool_or(i, c, conv_panel, pool_panel):
        return jnp.where(step(i, c) < n_conv, conv_panel, pool_panel)

    return pl.pallas_call(
        functools.partial(
            _mixer_kernel, n_tiles=n_tiles, ts=ts, tiles_per_seq=tiles_per_seq, n_conv=n_conv),
        grid=(n_tiles + 1, n_steps),
        in_specs=[
            pl.BlockSpec((ts, MIXER_SLAB_COLS), lambda i, c: (jnp.maximum(i - 1, 0), slab_col(c))),
            pl.BlockSpec((ts, MIXER_SLAB_COLS), lambda i, c: (next_tile(i), slab_col(c))),
            pl.BlockSpec((HALO, MIXER_SLAB_COLS),
                         lambda i, c: (jnp.maximum(next_tile(i) * (ts // HALO) - 1, 0), slab_col(c))),
            pl.BlockSpec((HALO, MIXER_SLAB_COLS), lambda i, c: (0, slab_col(c))),
            pl.BlockSpec((None, d, tc), lambda i, c: (
                pool_or(i, c, step(i, c), 3 * n_conv + 2 * group(i, c)), 0, 0)),
            pl.BlockSpec((None, d, tc), lambda i, c: (
                pool_or(i, c, n_conv + step(i, c), 3 * n_conv + 2 * group(i, c) + 1), 0, 0)),
            pl.BlockSpec((None, d, tc), lambda i, c: (
                2 * n_conv + jnp.minimum(step(i, c), n_conv - 1), 0, 0)),
            pl.BlockSpec((tc, d), lambda i, c: (step(i, c), 0)),
            pl.BlockSpec((None, pg, pg), lambda i, c: (group(i, c), 0, 0)),
            pl.BlockSpec((CONV_K, tc), lambda i, c: (0, jnp.minimum(step(i, c), n_conv - 1))),
            pl.BlockSpec((1, pg), lambda i, c: (0, group(i, c))),
            pl.BlockSpec((1, d), lambda i, c: (0, 0)),
            pl.BlockSpec((1, d), lambda i, c: (0, 0)),
        ],
        out_specs=pl.BlockSpec(memory_space=pl.ANY),
        out_shape=jax.ShapeDtypeStruct((m, d), F32),
        scratch_shapes=[
            pltpu.VMEM((ts, d), F32),
            pltpu.VMEM((2, HALO + ts, d), BF16),
            pltpu.VMEM((ts, pg), BF16),
            pltpu.SemaphoreType.DMA((OUT_COPIES_IN_FLIGHT,)),
        ],
        compiler_params=pltpu.CompilerParams(
            dimension_semantics=("arbitrary", "arbitrary"),
            vmem_limit_bytes=V7X_VMEM_LIMIT_BYTES,
        ),
        name="mixer_ln",
    )(h, h, h, meta_h, w_in, w_in, w_in, w_out, pool_w, conv_w, pool_scale, ln_g, ln_b)


def kernel(x, meta_tokens, ffn1_w_gu, ffn1_w_down, ln1_g, ln1_b, w_in, conv_w, pool_w, pool_scale,
           w_out, ln2_g, ln2_b, ffn2_w_gu, ffn2_w_down, ln3_g, ln3_b):
    b, seq, d = x.shape
    assert meta_tokens.shape == (N_META, d) and N_META == HALO
    assert ffn1_w_gu.shape[0] == DEPTH
    tm = 512

    h = x.reshape(b * seq, d)
    meta = meta_tokens.astype(x.dtype)
    for i in range(DEPTH):
        g1, b1 = ln1_g[i][None].astype(F32), ln1_b[i][None].astype(F32)
        g2, b2 = ln2_g[i][None].astype(F32), ln2_b[i][None].astype(F32)
        g3, b3 = ln3_g[i][None].astype(F32), ln3_b[i][None].astype(F32)
        n_groups, pg, _ = pool_w[i].shape
        nf = ffn1_w_down[i].shape[0] // V7X_MXU_COLS

        h1_head, meta1, wg1, wu1, wd1 = _ffn_ln_head(
            h, meta, ffn1_w_gu[i], ffn1_w_down[i], g1, b1, tm=tm)
        jobs = (
            _CastJob(ffn2_w_gu[i], (32, ffn2_w_gu[i].shape[1] // 2), 0, panel=V7X_MXU_COLS),
            _CastJob(ffn2_w_down[i], (64, d), 0),
            _CastJob(w_out[i], (64, d), ffn2_w_down[i].shape[0] // 64),
            _CastJob(w_in[i], (32, w_in[i].shape[1]), 0, panel=V7X_MXU_COLS),
            _CastJob(pool_w[i].reshape(n_groups * pg, pg), (128, pg), w_in[i].shape[0] // 32),
        )
        h1, (wgu2, wd2, w_out_b, w_in_b, pool_w_b) = _ffn_ln(
            h, wg1, wu1, 0, wd1, g1, b1, tm=768, tail=h1_head, cast_jobs=jobs)
        pool_w_b = pool_w_b.reshape(n_groups, pg, pg)

        h2 = _mixer_ln(h1, meta1, w_in_b, conv_w[i], pool_w_b, pool_scale[i][None], w_out_b, g2, b2,
                       ts=1024, seq=seq)
        h, _ = _ffn_ln(h2, wgu2, wgu2, nf, wd2, g3, b3, tm=1024)
        assert DEPTH == 1
    return h.reshape(b, seq, d)
```

```python
import functools
from typing import NamedTuple

import jax
import jax.numpy as jnp
from jax import lax
from jax.experimental import pallas as pl
from jax.experimental.pallas import tpu as pltpu

N_META = 16
CONV_K = 3
POOL_WINDOWS = (2, 4, 8, 16)
LN_EPS = 1e-5
DEPTH = 1
ALPHA = (2.0 * DEPTH) ** 0.25

BF16 = jnp.bfloat16
F32 = jnp.float32

V7X_MXU_COLS = 256
V7X_VMEM_LIMIT_BYTES = 60 * 1024 * 1024
HALO = 16
LN_ROWS = 8
LN_UNROLL = 16
FFN_DOWN_COLS = 512
X_SLAB_COLS = 128
MIXER_SLAB_COLS = 256
OUT_COPIES_IN_FLIGHT = 4


def _layer_norm_rows(y, g, b):
    mu = jnp.mean(y, axis=-1, keepdims=True)
    yc = y - mu
    var = jnp.mean(yc * yc, axis=-1, keepdims=True)
    return yc * lax.rsqrt(var + LN_EPS) * g + b


def _ln_group_rows(rows):
    return min(LN_ROWS * LN_UNROLL, rows)


def _layer_norm_group(o_ref, r0, res_ref, branch_scale, g_ref, b_ref):
    group = _ln_group_rows(o_ref.shape[0])
    outs = []
    for u in range(0, group, LN_ROWS):
        sub = pl.ds(r0 + u, LN_ROWS)
        y = o_ref[sub, :]
        if branch_scale is not None:
            y = branch_scale * y
        if res_ref is not None:
            y = ALPHA * res_ref[sub, :] + y
        outs.append(_layer_norm_rows(y, g_ref[...], b_ref[...]))
    for k, u in enumerate(range(0, group, LN_ROWS)):
        o_ref[pl.ds(r0 + u, LN_ROWS), :] = outs[k]


def _residual_layer_norm(o_ref, res_ref, branch_scale, g_ref, b_ref):
    rows = o_ref.shape[0]
    group = _ln_group_rows(rows)

    def body(r, carry):
        _layer_norm_group(o_ref, pl.multiple_of(r * group, group), res_ref, branch_scale, g_ref, b_ref)
        return carry

    lax.fori_loop(0, rows // group, body, 0)


def _swiglu_chunk(xb, wg, wu):
    gate = jnp.dot(xb, wg, preferred_element_type=F32)
    up = jnp.dot(xb, wu, preferred_element_type=F32)
    return (gate * jax.nn.sigmoid(gate) * up).astype(BF16)


class _CastJob(NamedTuple):
    src: jax.Array
    block: tuple
    start: int
    panel: int = 0

    @property
    def n_blocks(self):
        return (self.src.shape[0] // self.block[0]) * (self.src.shape[1] // self.block[1])

    @property
    def out_shape(self):
        rows, cols = self.src.shape
        shape = (cols // self.panel, rows, self.panel) if self.panel else (rows, cols)
        return jax.ShapeDtypeStruct(shape, BF16)

    def _block_index(self, steps_per_row_tile, i, j):
        col_blocks = self.src.shape[1] // self.block[1]
        s = jnp.clip(i * steps_per_row_tile + j - self.start, 0, self.n_blocks - 1)
        return s // col_blocks, s % col_blocks

    def in_spec(self, steps_per_row_tile):
        return pl.BlockSpec(self.block, functools.partial(self._block_index, steps_per_row_tile))

    def out_spec(self, steps_per_row_tile):
        if not self.panel:
            return self.in_spec(steps_per_row_tile)
        br, bc = self.block

        def index_map(i, j):
            rb, cb = self._block_index(steps_per_row_tile, i, j)
            return (cb, rb, 0)

        return pl.BlockSpec((bc // self.panel, br, self.panel), index_map)


def _cast_block(src_ref, dst_ref):
    if len(dst_ref.shape) == 2:
        dst_ref[...] = src_ref[...].astype(BF16)
    else:
        panel = dst_ref.shape[2]
        for p in range(dst_ref.shape[0]):
            dst_ref[p] = src_ref[:, p * panel:(p + 1) * panel].astype(BF16)


def _ffn_kernel(*refs, n_tiles, tm, n_cast):
    xres_ref, xnext_ref, wg_ref, wu_ref, wd_ref, g_ref, b_ref = refs[:7]
    cast_in = refs[7:7 + n_cast]
    n_in = len(refs) - (5 + n_cast)
    tail_hbm = refs[7 + n_cast] if n_in > 7 + n_cast else None
    out_hbm = refs[n_in]
    cast_out = refs[n_in + 1:n_in + 1 + n_cast]
    acc_ref, xb_ref, sem, tail_sem = refs[n_in + 1 + n_cast:]
    i = pl.program_id(0)
    j = pl.program_id(1)
    nf = pl.num_programs(1)
    d = acc_ref.shape[1]
    n_slabs = d // X_SLAB_COLS
    slab = pl.ds(pl.multiple_of(jnp.minimum(j, n_slabs - 1) * X_SLAB_COLS, X_SLAB_COLS), X_SLAB_COLS)

    if tail_hbm is not None:
        tail_copy = pltpu.make_async_copy(
            tail_hbm, out_hbm.at[pl.ds(n_tiles * tm, tail_hbm.shape[0])], tail_sem)

        @pl.when(jnp.logical_and(i == 0, j == 0))
        def _():
            tail_copy.start()

        @pl.when(jnp.logical_and(i == n_tiles, j == nf - 1))
        def _():
            tail_copy.wait()

    @pl.when(jnp.logical_and(i < n_tiles, j < n_slabs))
    def _():
        xb_ref.at[i % 2][:, slab] = xnext_ref[...].astype(BF16)

    @pl.when(i > 0)
    def _():
        @pl.when(j == 0)
        def _():
            acc_ref[...] = jnp.zeros_like(acc_ref)

        @pl.when(j < n_slabs)
        def _():
            acc_ref[:, slab] += (ALPHA / 0.5) * xres_ref[...]

        act = _swiglu_chunk(xb_ref[(i - 1) % 2], wg_ref[...], wu_ref[...])
        fillers = list(zip(cast_in, cast_out))
        for n0 in range(0, d, FFN_DOWN_COLS):
            cols = slice(n0, n0 + FFN_DOWN_COLS)
            acc_ref[:, cols] += jnp.dot(act, wd_ref[:, cols], preferred_element_type=F32)
            if fillers:
                _cast_block(*fillers.pop(0))
        assert not fillers

        @pl.when(j == nf - 1)
        def _():
            _layer_norm_and_write_back(acc_ref, out_hbm, sem, (i - 1) * tm, 0.5, g_ref, b_ref)


def _layer_norm_and_write_back(acc_ref, out_hbm, sem, row0, branch_scale, g_ref, b_ref):
    rows = acc_ref.shape[0]
    group = _ln_group_rows(rows)
    n_groups = rows // group
    in_flight = min(OUT_COPIES_IN_FLIGHT, n_groups)

    def group_copy(k):
        r0 = pl.multiple_of(k * group, group)
        return pltpu.make_async_copy(
            acc_ref.at[pl.ds(r0, group)], out_hbm.at[pl.ds(row0 + r0, group)], sem.at[k % in_flight])

    def body(k, carry):
        @pl.when(k >= in_flight)
        def _():
            group_copy(k - in_flight).wait()

        _layer_norm_group(acc_ref, pl.multiple_of(k * group, group), None, branch_scale, g_ref, b_ref)
        group_copy(k).start()
        return carry

    lax.fori_loop(0, n_groups, body, 0)
    for k in range(n_groups - in_flight, n_groups):
        group_copy(k).wait()


def _ffn_ln(x, wg, wu, up_block_offset, w_down, ln_g, ln_b, *, tm, tail=None, cast_jobs=()):
    m, d = x.shape
    f = w_down.shape[0]
    tf = wg.shape[2]
    n_tiles = (m - (0 if tail is None else tail.shape[0])) // tm
    assert n_tiles * tm + (0 if tail is None else tail.shape[0]) == m
    assert f % tf == 0 and tf == V7X_MXU_COLS and tm % _ln_group_rows(tm) == 0
    nf = f // tf
    n_slabs = d // X_SLAB_COLS
    assert n_slabs <= nf
    for job in cast_jobs:
        assert job.start + job.n_blocks <= n_tiles * nf

    def weight_chunk(i, j):
        return jnp.where(i == 0, 0, j)

    def slab_col(j):
        return jnp.minimum(j, n_slabs - 1)

    shifted_jobs = [job._replace(start=job.start + nf) for job in cast_jobs]
    operands = [x, x, wg, wu, w_down, ln_g, ln_b] + [job.src for job in cast_jobs]
    if tail is not None:
        operands.append(tail)
    outs = pl.pallas_call(
        functools.partial(_ffn_kernel, n_tiles=n_tiles, tm=tm, n_cast=len(cast_jobs)),
        grid=(n_tiles + 1, nf),
        in_specs=[
            pl.BlockSpec((tm, X_SLAB_COLS), lambda i, j: (jnp.maximum(i - 1, 0), slab_col(j))),
            pl.BlockSpec((tm, X_SLAB_COLS), lambda i, j: (jnp.minimum(i, n_tiles - 1), slab_col(j))),
            pl.BlockSpec((None, d, tf), lambda i, j: (weight_chunk(i, j), 0, 0)),
            pl.BlockSpec((None, d, tf), lambda i, j: (up_block_offset + weight_chunk(i, j), 0, 0)),
            pl.BlockSpec((tf, d), lambda i, j: (weight_chunk(i, j), 0)),
            pl.BlockSpec((1, d), lambda i, j: (0, 0)),
            pl.BlockSpec((1, d), lambda i, j: (0, 0)),
        ] + [job.in_spec(nf) for job in shifted_jobs]
        + ([pl.BlockSpec(memory_space=pl.ANY)] if tail is not None else []),
        out_specs=[pl.BlockSpec(memory_space=pl.ANY)] + [job.out_spec(nf) for job in shifted_jobs],
        out_shape=[jax.ShapeDtypeStruct((m, d), F32)] + [job.out_shape for job in cast_jobs],
        scratch_shapes=[
            pltpu.VMEM((tm, d), F32),
            pltpu.VMEM((2, tm, d), BF16),
            pltpu.SemaphoreType.DMA((OUT_COPIES_IN_FLIGHT,)),
            pltpu.SemaphoreType.DMA(()),
        ],
        compiler_params=pltpu.CompilerParams(
            dimension_semantics=("arbitrary", "arbitrary"),
            vmem_limit_bytes=V7X_VMEM_LIMIT_BYTES,
        ),
        name="ffn_ln",
    )(*operands)
    return outs[0], tuple(outs[1:])


def _ffn_head_kernel(x_ref, meta_ref, wg_ref, wu_ref, wd_ref, g_ref, b_ref,
                     o_ref, om_ref, wgb_ref, wub_ref, wdb_ref, xb_ref):
    j = pl.program_id(0)
    n_meta = meta_ref.shape[0]

    @pl.when(j == 0)
    def _():
        xb_ref[0:n_meta, :] = meta_ref[...].astype(BF16)
        xb_ref[n_meta:, :] = x_ref[...].astype(BF16)
        o_ref[...] = jnp.zeros_like(o_ref)
        om_ref[...] = jnp.zeros_like(om_ref)

    wgb_ref[...] = wg_ref[...].astype(BF16)
    wub_ref[...] = wu_ref[...].astype(BF16)
    wdb_ref[...] = wd_ref[...].astype(BF16)
    act = _swiglu_chunk(xb_ref[...], wgb_ref[...], wub_ref[...])
    d = o_ref.shape[1]
    for n0 in range(0, d, FFN_DOWN_COLS):
        cols = slice(n0, n0 + FFN_DOWN_COLS)
        part = jnp.dot(act, wdb_ref[:, cols], preferred_element_type=F32)
        om_ref[:, cols] += part[0:n_meta, :]
        o_ref[:, cols] += part[n_meta:, :]

    @pl.when(j == pl.num_programs(0) - 1)
    def _():
        _residual_layer_norm(om_ref, meta_ref, 0.5, g_ref, b_ref)
        _residual_layer_norm(o_ref, x_ref, 0.5, g_ref, b_ref)


def _ffn_ln_head(x, meta, w_gu, w_down, ln_g, ln_b, *, tm):
    m, d = x.shape
    n_meta = meta.shape[0]
    f = w_down.shape[0]
    tf = V7X_MXU_COLS
    nf = f // tf
    assert m % tm == 0
    last = m // tm - 1
    resident = pl.Buffered(1)
    return pl.pallas_call(
        _ffn_head_kernel,
        grid=(nf,),
        in_specs=[
            pl.BlockSpec((tm, d), lambda j: (last, 0), pipeline_mode=resident),
            pl.BlockSpec((n_meta, d), lambda j: (0, 0), pipeline_mode=resident),
            pl.BlockSpec((d, tf), lambda j: (0, j)),
            pl.BlockSpec((d, tf), lambda j: (0, nf + j)),
            pl.BlockSpec((tf, d), lambda j: (j, 0)),
            pl.BlockSpec((1, d), lambda j: (0, 0)),
            pl.BlockSpec((1, d), lambda j: (0, 0)),
        ],
        out_specs=[
            pl.BlockSpec((tm, d), lambda j: (0, 0), pipeline_mode=resident),
            pl.BlockSpec((n_meta, d), lambda j: (0, 0), pipeline_mode=resident),
            pl.BlockSpec((None, d, tf), lambda j: (j, 0, 0)),
            pl.BlockSpec((None, d, tf), lambda j: (j, 0, 0)),
            pl.BlockSpec((tf, d), lambda j: (j, 0)),
        ],
        out_shape=[
            jax.ShapeDtypeStruct((tm, d), F32),
            jax.ShapeDtypeStruct((n_meta, d), F32),
            jax.ShapeDtypeStruct((nf, d, tf), BF16),
            jax.ShapeDtypeStruct((nf, d, tf), BF16),
            jax.ShapeDtypeStruct((f, d), BF16),
        ],
        scratch_shapes=[pltpu.VMEM((n_meta + tm, d), BF16)],
        compiler_params=pltpu.CompilerParams(
            dimension_semantics=("arbitrary",),
            vmem_limit_bytes=V7X_VMEM_LIMIT_BYTES,
        ),
        name="ffn_ln_head",
    )(x, meta, w_gu, w_gu, w_down, ln_g, ln_b)


def _mixer_kernel(hres_ref, hnext_ref, halo_ref, meta_ref, wa_ref, wb_ref, wc_ref, wo_ref, pw_ref,
                  cw_ref, ps_ref, g_ref, b_ref, out_hbm, acc_ref, xb_ref, yp_ref, sem,
                  *, n_tiles, ts, tiles_per_seq, n_conv):
    i = pl.program_id(0)
    c = pl.program_id(1)
    d = acc_ref.shape[1]
    tc = wa_ref.shape[1]
    n_slabs = d // MIXER_SLAB_COLS
    slab = pl.ds(
        pl.multiple_of(jnp.minimum(c, n_slabs - 1) * MIXER_SLAB_COLS, MIXER_SLAB_COLS), MIXER_SLAB_COLS)

    def round_next_slab():
        first = (i % tiles_per_seq) == 0
        nxt = xb_ref.at[i % 2]
        nxt[0:HALO, slab] = jnp.where(first, meta_ref[...], halo_ref[...]).astype(BF16)
        nxt[HALO:, slab] = hnext_ref[...].astype(BF16)

    def accumulate(y):
        for n0 in range(0, d, FFN_DOWN_COLS):
            cols = slice(n0, n0 + FFN_DOWN_COLS)
            acc_ref[:, cols] += jnp.dot(y, wo_ref[:, cols], preferred_element_type=F32)

    @pl.when(jnp.logical_and(i < n_tiles, c < n_slabs))
    def _():
        round_next_slab()

    @pl.when(i > 0)
    def _():
        @pl.when(c == 0)
        def _():
            acc_ref[...] = jnp.zeros_like(acc_ref)

        @pl.when(c < n_slabs)
        def _():
            acc_ref[:, slab] += ALPHA * hres_ref[...]

        pool_step = c - n_conv
        group = pool_step // 2

        @pl.when(c < n_conv)
        def _():
            half = HALO + (ts // 2)
            hb_ref = xb_ref.at[(i - 1) % 2]
            tops = [jnp.dot(hb_ref[0:half, :], w[...], preferred_element_type=F32)
                    for w in (wa_ref, wb_ref, wc_ref)]
            bots = [jnp.dot(hb_ref[half:, :], w[...], preferred_element_type=F32)
                    for w in (wa_ref, wb_ref, wc_ref)]
            gate_b, gate_c, x_in = [jnp.concatenate(tb, axis=0) for tb in zip(tops, bots)]
            v = gate_c * x_in
            conv = pltpu.roll(v, 2, 0) * cw_ref[0:1, :]
            conv = conv + pltpu.roll(v, 1, 0) * cw_ref[1:2, :]
            conv = conv + v * cw_ref[2:3, :]
            accumulate((gate_b * conv)[HALO:, :].astype(BF16))

        @pl.when(jnp.logical_and(pool_step >= 0, pool_step % 2 == 0))
        def _():
            hb = xb_ref[(i - 1) % 2]
            z = jnp.concatenate([jnp.dot(hb, wa_ref[...], preferred_element_type=F32),
                                 jnp.dot(hb, wb_ref[...], preferred_element_type=F32)], axis=1)
            s = z
            for level in range(len(POOL_WINDOWS)):
                doubled = s + pltpu.roll(s, 1 << level, 0)
                s = doubled if level == 0 else jnp.where(level <= group, doubled, s)
            inv_w = jnp.where(group == 0, 1.0 / POOL_WINDOWS[0],
                              jnp.where(group == 1, 1.0 / POOL_WINDOWS[1],
                                        jnp.where(group == 2, 1.0 / POOL_WINDOWS[2],
                                                  1.0 / POOL_WINDOWS[3])))
            diff = (s * inv_w.astype(F32) - z)[HALO:, :].astype(BF16)
            y = (jnp.dot(diff, pw_ref[...], preferred_element_type=F32) * ps_ref[...]).astype(BF16)
            yp_ref[...] = y
            accumulate(y[:, 0:tc])

        @pl.when(jnp.logical_and(pool_step >= 0, pool_step % 2 == 1))
        def _():
            accumulate(yp_ref[:, tc:])

        @pl.when(c == pl.num_programs(1) - 1)
        def _():
            _layer_norm_and_write_back(acc_ref, out_hbm, sem, (i - 1) * ts, None, g_ref, b_ref)


def _mixer_ln(h, meta_h, w_in, conv_w, pool_w, pool_scale, w_out, ln_g, ln_b, *, ts, seq):
    m, d = h.shape
    cc = conv_w.shape[1]
    n_groups, pg, _ = pool_w.shape
    n_panels, _, tc = w_in.shape
    n_conv = cc // tc
    assert pg == 2 * tc and n_panels == 3 * n_conv + 2 * n_groups and w_out.shape[0] == cc + n_groups * pg
    n_steps = n_conv + 2 * n_groups
    n_tiles = m // ts
    tiles_per_seq = seq // ts
    n_slabs = d // MIXER_SLAB_COLS
    assert m % ts == 0 and seq % ts == 0 and n_slabs <= n_steps and ts % _ln_group_rows(ts) == 0

    def step(i, c):
        return jnp.where(i == 0, 0, c)

    def group(i, c):
        return jnp.maximum(step(i, c) - n_conv, 0) // 2

    def slab_col(c):
        return jnp.minimum(c, n_slabs - 1)

    def next_tile(i):
        return jnp.minimum(i, n_tiles - 1)

    def pool_or(i, c, conv_panel, pool_panel):
        return jnp.where(step(i, c) < n_conv, conv_panel, pool_panel)

    return pl.pallas_call(
        functools.partial(
            _mixer_kernel, n_tiles=n_tiles, ts=ts, tiles_per_seq=tiles_per_seq, n_conv=n_conv),
        grid=(n_tiles + 1, n_steps),
        in_specs=[
            pl.BlockSpec((ts, MIXER_SLAB_COLS), lambda i, c: (jnp.maximum(i - 1, 0), slab_col(c))),
            pl.BlockSpec((ts, MIXER_SLAB_COLS), lambda i, c: (next_tile(i), slab_col(c))),
            pl.BlockSpec((HALO, MIXER_SLAB_COLS),
                         lambda i, c: (jnp.maximum(next_tile(i) * (ts // HALO) - 1, 0), slab_col(c))),
            pl.BlockSpec((HALO, MIXER_SLAB_COLS), lambda i, c: (0, slab_col(c))),
            pl.BlockSpec((None, d, tc), lambda i, c: (
                pool_or(i, c, step(i, c), 3 * n_conv + 2 * group(i, c)), 0, 0)),
            pl.BlockSpec((None, d, tc), lambda i, c: (
                pool_or(i, c, n_conv + step(i, c), 3 * n_conv + 2 * group(i, c) + 1), 0, 0)),
            pl.BlockSpec((None, d, tc), lambda i, c: (
                2 * n_conv + jnp.minimum(step(i, c), n_conv - 1), 0, 0)),
            pl.BlockSpec((tc, d), lambda i, c: (step(i, c), 0)),
            pl.BlockSpec((None, pg, pg), lambda i, c: (group(i, c), 0, 0)),
            pl.BlockSpec((CONV_K, tc), lambda i, c: (0, jnp.minimum(step(i, c), n_conv - 1))),
            pl.BlockSpec((1, pg), lambda i, c: (0, group(i, c))),
            pl.BlockSpec((1, d), lambda i, c: (0, 0)),
            pl.BlockSpec((1, d), lambda i, c: (0, 0)),
        ],
        out_specs=pl.BlockSpec(memory_space=pl.ANY),
        out_shape=jax.ShapeDtypeStruct((m, d), F32),
        scratch_shapes=[
            pltpu.VMEM((ts, d), F32),
            pltpu.VMEM((2, HALO + ts, d), BF16),
            pltpu.VMEM((ts, pg), BF16),
            pltpu.SemaphoreType.DMA((OUT_COPIES_IN_FLIGHT,)),
        ],
        compiler_params=pltpu.CompilerParams(
            dimension_semantics=("arbitrary", "arbitrary"),
            vmem_limit_bytes=V7X_VMEM_LIMIT_BYTES,
        ),
        name="mixer_ln",
    )(h, h, h, meta_h, w_in, w_in, w_in, w_out, pool_w, conv_w, pool_scale, ln_g, ln_b)


def kernel(x, meta_tokens, ffn1_w_gu, ffn1_w_down, ln1_g, ln1_b, w_in, conv_w, pool_w, pool_scale,
           w_out, ln2_g, ln2_b, ffn2_w_gu, ffn2_w_down, ln3_g, ln3_b):
    b, seq, d = x.shape
    assert meta_tokens.shape == (N_META, d) and N_META == HALO
    assert ffn1_w_gu.shape[0] == DEPTH
    tm = 512

    h = x.reshape(b * seq, d)
    meta = meta_tokens.astype(x.dtype)
    for i in range(DEPTH):
        g1, b1 = ln1_g[i][None].astype(F32), ln1_b[i][None].astype(F32)
        g2, b2 = ln2_g[i][None].astype(F32), ln2_b[i][None].astype(F32)
        g3, b3 = ln3_g[i][None].astype(F32), ln3_b[i][None].astype(F32)
        n_groups, pg, _ = pool_w[i].shape
        nf = ffn1_w_down[i].shape[0] // V7X_MXU_COLS

        h1_head, meta1, wg1, wu1, wd1 = _ffn_ln_head(
            h, meta, ffn1_w_gu[i], ffn1_w_down[i], g1, b1, tm=tm)
        jobs = (
            _CastJob(ffn2_w_gu[i], (32, ffn2_w_gu[i].shape[1] // 2), 0, panel=V7X_MXU_COLS),
            _CastJob(ffn2_w_down[i], (64, d), 0),
            _CastJob(w_out[i], (64, d), ffn2_w_down[i].shape[0] // 64),
            _CastJob(w_in[i], (32, w_in[i].shape[1]), 0, panel=V7X_MXU_COLS),
            _CastJob(pool_w[i].reshape(n_groups * pg, pg), (128, pg), w_in[i].shape[0] // 32),
        )
        h1, (wgu2, wd2, w_out_b, w_in_b, pool_w_b) = _ffn_ln(
            h, wg1, wu1, 0, wd1, g1, b1, tm=768, tail=h1_head, cast_jobs=jobs)
        pool_w_b = pool_w_b.reshape(n_groups, pg, pg)

        h2 = _mixer_ln(h1, meta1, w_in_b, conv_w[i], pool_w_b, pool_scale[i][None], w_out_b, g2, b2,
                       ts=1024, seq=seq)
        h, _ = _ffn_ln(h2, wgu2, wgu2, nf, wd2, g3, b3, tm=1024)
        assert DEPTH == 1
    return h.reshape(b, seq, d)
```

```python
import functools
from typing import NamedTuple

import jax
import jax.numpy as jnp
from jax import lax
from jax.experimental import pallas as pl
from jax.experimental.pallas import tpu as pltpu

N_META = 16
CONV_K = 3
POOL_WINDOWS = (2, 4, 8, 16)
LN_EPS = 1e-5
DEPTH = 1
ALPHA = (2.0 * DEPTH) ** 0.25

BF16 = jnp.bfloat16
F32 = jnp.float32

V7X_MXU_COLS = 256
V7X_VMEM_LIMIT_BYTES = 60 * 1024 * 1024
HALO = 16
LN_ROWS = 8
LN_UNROLL = 16
FFN_DOWN_COLS = 512
X_SLAB_COLS = 128
MIXER_SLAB_COLS = 256
OUT_COPIES_IN_FLIGHT = 4


def _layer_norm_rows(y, g, b):
    mu = jnp.mean(y, axis=-1, keepdims=True)
    yc = y - mu
    var = jnp.mean(yc * yc, axis=-1, keepdims=True)
    return yc * lax.rsqrt(var + LN_EPS) * g + b


def _ln_group_rows(rows):
    return min(LN_ROWS * LN_UNROLL, rows)


def _layer_norm_group(o_ref, r0, res_ref, branch_scale, g_ref, b_ref):
    group = _ln_group_rows(o_ref.shape[0])
    outs = []
    for u in range(0, group, LN_ROWS):
        sub = pl.ds(r0 + u, LN_ROWS)
        y = o_ref[sub, :]
        if branch_scale is not None:
            y = branch_scale * y
        if res_ref is not None:
            y = ALPHA * res_ref[sub, :] + y
        outs.append(_layer_norm_rows(y, g_ref[...], b_ref[...]))
    for k, u in enumerate(range(0, group, LN_ROWS)):
        o_ref[pl.ds(r0 + u, LN_ROWS), :] = outs[k]


def _residual_layer_norm(o_ref, res_ref, branch_scale, g_ref, b_ref):
    rows = o_ref.shape[0]
    group = _ln_group_rows(rows)

    def body(r, carry):
        _layer_norm_group(o_ref, pl.multiple_of(r * group, group), res_ref, branch_scale, g_ref, b_ref)
        return carry

    lax.fori_loop(0, rows // group, body, 0)


def _swiglu_chunk(xb, wgu):
    tf = wgu.shape[1] // 2
    gu = jnp.dot(xb, wgu, preferred_element_type=F32)
    gate, up = gu[:, :tf], gu[:, tf:]
    return (gate * jax.nn.sigmoid(gate) * up).astype(BF16)


class _CastJob(NamedTuple):
    src: jax.Array
    block: tuple
    start: int
    panel: int = 0

    @property
    def n_blocks(self):
        return (self.src.shape[0] // self.block[0]) * (self.src.shape[1] // self.block[1])

    @property
    def out_shape(self):
        rows, cols = self.src.shape
        col_blocks = cols // self.block[1]
        shape = (self.block[1] // self.panel, rows, col_blocks * self.panel) if self.panel else (rows, cols)
        return jax.ShapeDtypeStruct(shape, BF16)

    def _block_index(self, steps_per_row_tile, i, j):
        col_blocks = self.src.shape[1] // self.block[1]
        s = jnp.clip(i * steps_per_row_tile + j - self.start, 0, self.n_blocks - 1)
        return s // col_blocks, s % col_blocks

    def in_spec(self, steps_per_row_tile):
        return pl.BlockSpec(self.block, functools.partial(self._block_index, steps_per_row_tile))

    def out_spec(self, steps_per_row_tile):
        if not self.panel:
            return self.in_spec(steps_per_row_tile)
        br, bc = self.block

        def index_map(i, j):
            rb, cb = self._block_index(steps_per_row_tile, i, j)
            return (0, rb, cb)

        return pl.BlockSpec((bc // self.panel, br, self.panel), index_map)


def _cast_block(src_ref, dst_ref):
    if len(dst_ref.shape) == 2:
        dst_ref[...] = src_ref[...].astype(BF16)
    else:
        panel = dst_ref.shape[2]
        for p in range(dst_ref.shape[0]):
            dst_ref[p] = src_ref[:, p * panel:(p + 1) * panel].astype(BF16)


def _ffn_kernel(*refs, n_tiles, tm, cast_ranges):
    n_cast = len(cast_ranges)
    n_fixed = 6
    xres_ref, xnext_ref, wgu_ref, wd_ref, g_ref, b_ref = refs[:n_fixed]
    cast_in = refs[n_fixed:n_fixed + n_cast]
    n_in = len(refs) - (5 + n_cast)
    tail_hbm = refs[n_fixed + n_cast] if n_in > n_fixed + n_cast else None
    out_hbm = refs[n_in]
    cast_out = refs[n_in + 1:n_in + 1 + n_cast]
    acc_ref, xb_ref, sem, tail_sem = refs[n_in + 1 + n_cast:]
    i = pl.program_id(0)
    j = pl.program_id(1)
    nf = pl.num_programs(1)
    d = acc_ref.shape[1]
    n_slabs = d // X_SLAB_COLS
    slab = pl.ds(pl.multiple_of(jnp.minimum(j, n_slabs - 1) * X_SLAB_COLS, X_SLAB_COLS), X_SLAB_COLS)

    if tail_hbm is not None:
        tail_copy = pltpu.make_async_copy(
            tail_hbm, out_hbm.at[pl.ds(n_tiles * tm, tail_hbm.shape[0])], tail_sem)

        @pl.when(jnp.logical_and(i == 0, j == 0))
        def _():
            tail_copy.start()

        @pl.when(jnp.logical_and(i == n_tiles, j == nf - 1))
        def _():
            tail_copy.wait()

    @pl.when(jnp.logical_and(i < n_tiles, j < n_slabs))
    def _():
        xb_ref.at[i % 2][:, slab] = xnext_ref[...].astype(BF16)

    step = (i - 1) * nf + j
    for src_ref, dst_ref, (start, stop) in zip(cast_in, cast_out, cast_ranges):
        @pl.when(jnp.logical_and(step >= start, step < stop))
        def _(src_ref=src_ref, dst_ref=dst_ref):
            _cast_block(src_ref, dst_ref)

    @pl.when(i > 0)
    def _():
        @pl.when(j == 0)
        def _():
            acc_ref[...] = jnp.zeros_like(acc_ref)

        @pl.when(j < n_slabs)
        def _():
            acc_ref[:, slab] += (ALPHA / 0.5) * xres_ref[...]

        act = _swiglu_chunk(xb_ref[(i - 1) % 2], wgu_ref[...])
        for n0 in range(0, d, FFN_DOWN_COLS):
            cols = slice(n0, n0 + FFN_DOWN_COLS)
            acc_ref[:, cols] += jnp.dot(act, wd_ref[:, cols], preferred_element_type=F32)

        @pl.when(j == nf - 1)
        def _():
            _layer_norm_and_write_back(acc_ref, out_hbm, sem, (i - 1) * tm, 0.5, g_ref, b_ref)


def _layer_norm_and_write_back(acc_ref, out_hbm, sem, row0, branch_scale, g_ref, b_ref):
    rows = acc_ref.shape[0]
    group = _ln_group_rows(rows)
    n_groups = rows // group
    in_flight = min(OUT_COPIES_IN_FLIGHT, n_groups)

    def group_copy(k):
        r0 = pl.multiple_of(k * group, group)
        return pltpu.make_async_copy(
            acc_ref.at[pl.ds(r0, group)], out_hbm.at[pl.ds(row0 + r0, group)], sem.at[k % in_flight])

    def body(k, carry):
        @pl.when(k >= in_flight)
        def _():
            group_copy(k - in_flight).wait()

        _layer_norm_group(acc_ref, pl.multiple_of(k * group, group), None, branch_scale, g_ref, b_ref)
        group_copy(k).start()
        return carry

    lax.fori_loop(0, n_groups, body, 0)
    for k in range(n_groups - in_flight, n_groups):
        group_copy(k).wait()


def _ffn_ln(x, wgu, w_down, ln_g, ln_b, *, tm, tail=None, cast_jobs=()):
    m, d = x.shape
    f = w_down.shape[0]
    tf = wgu.shape[2] // 2
    n_tiles = (m - (0 if tail is None else tail.shape[0])) // tm
    assert n_tiles * tm + (0 if tail is None else tail.shape[0]) == m
    assert f % tf == 0 and tf == V7X_MXU_COLS and tm % _ln_group_rows(tm) == 0
    nf = f // tf
    n_slabs = d // X_SLAB_COLS
    assert n_slabs <= nf
    for job in cast_jobs:
        assert job.start + job.n_blocks <= n_tiles * nf

    def weight_chunk(i, j):
        return jnp.where(i == 0, 0, j)

    def slab_col(j):
        return jnp.minimum(j, n_slabs - 1)

    shifted_jobs = [job._replace(start=job.start + nf) for job in cast_jobs]
    operands = [x, x, wgu, w_down, ln_g, ln_b] + [job.src for job in cast_jobs]
    if tail is not None:
        operands.append(tail)
    outs = pl.pallas_call(
        functools.partial(
            _ffn_kernel, n_tiles=n_tiles, tm=tm,
            cast_ranges=tuple((job.start, job.start + job.n_blocks) for job in cast_jobs)),
        grid=(n_tiles + 1, nf),
        in_specs=[
            pl.BlockSpec((tm, X_SLAB_COLS), lambda i, j: (jnp.maximum(i - 1, 0), slab_col(j))),
            pl.BlockSpec((tm, X_SLAB_COLS), lambda i, j: (jnp.minimum(i, n_tiles - 1), slab_col(j))),
            pl.BlockSpec((None, d, 2 * tf), lambda i, j: (weight_chunk(i, j), 0, 0)),
            pl.BlockSpec((tf, d), lambda i, j: (weight_chunk(i, j), 0)),
            pl.BlockSpec((1, d), lambda i, j: (0, 0)),
            pl.BlockSpec((1, d), lambda i, j: (0, 0)),
        ] + [job.in_spec(nf) for job in shifted_jobs]
        + ([pl.BlockSpec(memory_space=pl.ANY)] if tail is not None else []),
        out_specs=[pl.BlockSpec(memory_space=pl.ANY)] + [job.out_spec(nf) for job in shifted_jobs],
        out_shape=[jax.ShapeDtypeStruct((m, d), F32)] + [job.out_shape for job in cast_jobs],
        scratch_shapes=[
            pltpu.VMEM((tm, d), F32),
            pltpu.VMEM((2, tm, d), BF16),
            pltpu.SemaphoreType.DMA((OUT_COPIES_IN_FLIGHT,)),
            pltpu.SemaphoreType.DMA(()),
        ],
        compiler_params=pltpu.CompilerParams(
            dimension_semantics=("arbitrary", "arbitrary"),
            vmem_limit_bytes=V7X_VMEM_LIMIT_BYTES,
        ),
        name="ffn_ln",
    )(*operands)
    return outs[0], tuple(outs[1:])


def _ffn_head_kernel(x_ref, meta_ref, wg_ref, wu_ref, wd_ref, g_ref, b_ref,
                     o_ref, om_ref, wgub_ref, wdb_ref, xb_ref):
    j = pl.program_id(0)
    n_meta = meta_ref.shape[0]

    @pl.when(j == 0)
    def _():
        xb_ref[0:n_meta, :] = meta_ref[...].astype(BF16)
        xb_ref[n_meta:, :] = x_ref[...].astype(BF16)
        o_ref[...] = jnp.zeros_like(o_ref)
        om_ref[...] = jnp.zeros_like(om_ref)

    tf = wg_ref.shape[1]
    wgub_ref[:, 0:tf] = wg_ref[...].astype(BF16)
    wgub_ref[:, tf:] = wu_ref[...].astype(BF16)
    wdb_ref[...] = wd_ref[...].astype(BF16)
    act = _swiglu_chunk(xb_ref[...], wgub_ref[...])
    d = o_ref.shape[1]
    for n0 in range(0, d, FFN_DOWN_COLS):
        cols = slice(n0, n0 + FFN_DOWN_COLS)
        part = jnp.dot(act, wdb_ref[:, cols], preferred_element_type=F32)
        om_ref[:, cols] += part[0:n_meta, :]
        o_ref[:, cols] += part[n_meta:, :]

    @pl.when(j == pl.num_programs(0) - 1)
    def _():
        _residual_layer_norm(om_ref, meta_ref, 0.5, g_ref, b_ref)
        _residual_layer_norm(o_ref, x_ref, 0.5, g_ref, b_ref)


def _ffn_ln_head(x, meta, w_gu, w_down, ln_g, ln_b, *, tm):
    m, d = x.shape
    n_meta = meta.shape[0]
    f = w_down.shape[0]
    tf = V7X_MXU_COLS
    nf = f // tf
    assert m % tm == 0
    last = m // tm - 1
    resident = pl.Buffered(1)
    return pl.pallas_call(
        _ffn_head_kernel,
        grid=(nf,),
        in_specs=[
            pl.BlockSpec((tm, d), lambda j: (last, 0), pipeline_mode=resident),
            pl.BlockSpec((n_meta, d), lambda j: (0, 0), pipeline_mode=resident),
            pl.BlockSpec((d, tf), lambda j: (0, j)),
            pl.BlockSpec((d, tf), lambda j: (0, nf + j)),
            pl.BlockSpec((tf, d), lambda j: (j, 0)),
            pl.BlockSpec((1, d), lambda j: (0, 0)),
            pl.BlockSpec((1, d), lambda j: (0, 0)),
        ],
        out_specs=[
            pl.BlockSpec((tm, d), lambda j: (0, 0), pipeline_mode=resident),
            pl.BlockSpec((n_meta, d), lambda j: (0, 0), pipeline_mode=resident),
            pl.BlockSpec((None, d, 2 * tf), lambda j: (j, 0, 0)),
            pl.BlockSpec((tf, d), lambda j: (j, 0)),
        ],
        out_shape=[
            jax.ShapeDtypeStruct((tm, d), F32),
            jax.ShapeDtypeStruct((n_meta, d), F32),
            jax.ShapeDtypeStruct((nf, d, 2 * tf), BF16),
            jax.ShapeDtypeStruct((f, d), BF16),
        ],
        scratch_shapes=[pltpu.VMEM((n_meta + tm, d), BF16)],
        compiler_params=pltpu.CompilerParams(
            dimension_semantics=("arbitrary",),
            vmem_limit_bytes=V7X_VMEM_LIMIT_BYTES,
        ),
        name="ffn_ln_head",
    )(x, meta, w_gu, w_gu, w_down, ln_g, ln_b)


def _mixer_kernel(hres_ref, hnext_ref, halo_ref, meta_ref, wa_ref, wb_ref, wc_ref, wo_ref, pw_ref,
                  cw_ref, ps_ref, g_ref, b_ref, out_hbm, acc_ref, xb_ref, yp_ref, sem,
                  *, n_tiles, ts, tiles_per_seq, n_conv):
    i = pl.program_id(0)
    c = pl.program_id(1)
    d = acc_ref.shape[1]
    tc = wa_ref.shape[1]
    n_slabs = d // MIXER_SLAB_COLS
    slab = pl.ds(
        pl.multiple_of(jnp.minimum(c, n_slabs - 1) * MIXER_SLAB_COLS, MIXER_SLAB_COLS), MIXER_SLAB_COLS)

    def round_next_slab():
        first = (i % tiles_per_seq) == 0
        nxt = xb_ref.at[i % 2]
        nxt[0:HALO, slab] = jnp.where(first, meta_ref[...], halo_ref[...]).astype(BF16)
        nxt[HALO:, slab] = hnext_ref[...].astype(BF16)

    def accumulate(y):
        for n0 in range(0, d, FFN_DOWN_COLS):
            cols = slice(n0, n0 + FFN_DOWN_COLS)
            acc_ref[:, cols] += jnp.dot(y, wo_ref[:, cols], preferred_element_type=F32)

    @pl.when(jnp.logical_and(i < n_tiles, c < n_slabs))
    def _():
        round_next_slab()

    @pl.when(i > 0)
    def _():
        @pl.when(c == 0)
        def _():
            acc_ref[...] = jnp.zeros_like(acc_ref)

        @pl.when(c < n_slabs)
        def _():
            acc_ref[:, slab] += ALPHA * hres_ref[...]

        pool_step = c - n_conv
        group = pool_step // 2

        @pl.when(c < n_conv)
        def _():
            half = HALO + (ts // 2)
            hb_ref = xb_ref.at[(i - 1) % 2]
            tops = [jnp.dot(hb_ref[0:half, :], w[...], preferred_element_type=F32)
                    for w in (wa_ref, wb_ref, wc_ref)]
            bots = [jnp.dot(hb_ref[half:, :], w[...], preferred_element_type=F32)
                    for w in (wa_ref, wb_ref, wc_ref)]
            gate_b, gate_c, x_in = [jnp.concatenate(tb, axis=0) for tb in zip(tops, bots)]
            v = gate_c * x_in
            conv = pltpu.roll(v, 2, 0) * cw_ref[0:1, :]
            conv = conv + pltpu.roll(v, 1, 0) * cw_ref[1:2, :]
            conv = conv + v * cw_ref[2:3, :]
            accumulate((gate_b * conv)[HALO:, :].astype(BF16))

        @pl.when(jnp.logical_and(pool_step >= 0, pool_step % 2 == 0))
        def _():
            hb = xb_ref[(i - 1) % 2]
            z = jnp.concatenate([jnp.dot(hb, wa_ref[...], preferred_element_type=F32),
                                 jnp.dot(hb, wb_ref[...], preferred_element_type=F32)], axis=1)
            s = z
            for level in range(len(POOL_WINDOWS)):
                doubled = s + pltpu.roll(s, 1 << level, 0)
                s = doubled if level == 0 else jnp.where(level <= group, doubled, s)
            inv_w = jnp.where(group == 0, 1.0 / POOL_WINDOWS[0],
                              jnp.where(group == 1, 1.0 / POOL_WINDOWS[1],
                                        jnp.where(group == 2, 1.0 / POOL_WINDOWS[2],
                                                  1.0 / POOL_WINDOWS[3])))
            diff = (s * inv_w.astype(F32) - z)[HALO:, :].astype(BF16)
            y = (jnp.dot(diff, pw_ref[...], preferred_element_type=F32) * ps_ref[...]).astype(BF16)
            yp_ref[...] = y
            accumulate(y[:, 0:tc])

        @pl.when(jnp.logical_and(pool_step >= 0, pool_step % 2 == 1))
        def _():
            accumulate(yp_ref[:, tc:])

        @pl.when(c == pl.num_programs(1) - 1)
        def _():
            _layer_norm_and_write_back(acc_ref, out_hbm, sem, (i - 1) * ts, None, g_ref, b_ref)


def _mixer_ln(h, meta_h, w_in, conv_w, pool_w, pool_scale, w_out, ln_g, ln_b, *, ts, seq):
    m, d = h.shape
    cc = conv_w.shape[1]
    n_groups, pg, _ = pool_w.shape
    n_panels, _, tc = w_in.shape
    n_conv = cc // tc
    assert pg == 2 * tc and n_panels == 3 * n_conv + 2 * n_groups and w_out.shape[0] == cc + n_groups * pg
    n_steps = n_conv + 2 * n_groups
    n_tiles = m // ts
    tiles_per_seq = seq // ts
    n_slabs = d // MIXER_SLAB_COLS
    assert m % ts == 0 and seq % ts == 0 and n_slabs <= n_steps and ts % _ln_group_rows(ts) == 0

    def step(i, c):
        return jnp.where(i == 0, 0, c)

    def group(i, c):
        return jnp.maximum(step(i, c) - n_conv, 0) // 2

    def slab_col(c):
        return jnp.minimum(c, n_slabs - 1)

    def next_tile(i):
        return jnp.minimum(i, n_tiles - 1)

    def pool_or(i, c, conv_panel, pool_panel):
        return jnp.where(step(i, c) < n_conv, conv_panel, pool_panel)

    return pl.pallas_call(
        functools.partial(
            _mixer_kernel, n_tiles=n_tiles, ts=ts, tiles_per_seq=tiles_per_seq, n_conv=n_conv),
        grid=(n_tiles + 1, n_steps),
        in_specs=[
            pl.BlockSpec((ts, MIXER_SLAB_COLS), lambda i, c: (jnp.maximum(i - 1, 0), slab_col(c))),
            pl.BlockSpec((ts, MIXER_SLAB_COLS), lambda i, c: (next_tile(i), slab_col(c))),
            pl.BlockSpec((HALO, MIXER_SLAB_COLS),
                         lambda i, c: (jnp.maximum(next_tile(i) * (ts // HALO) - 1, 0), slab_col(c))),
            pl.BlockSpec((HALO, MIXER_SLAB_COLS), lambda i, c: (0, slab_col(c))),
            pl.BlockSpec((None, d, tc), lambda i, c: (
                pool_or(i, c, step(i, c), 3 * n_conv + 2 * group(i, c)), 0, 0)),
            pl.BlockSpec((None, d, tc), lambda i, c: (
                pool_or(i, c, n_conv + step(i, c), 3 * n_conv + 2 * group(i, c) + 1), 0, 0)),
            pl.BlockSpec((None, d, tc), lambda i, c: (
                2 * n_conv + jnp.minimum(step(i, c), n_conv - 1), 0, 0)),
            pl.BlockSpec((tc, d), lambda i, c: (step(i, c), 0)),
            pl.BlockSpec((None, pg, pg), lambda i, c: (group(i, c), 0, 0)),
            pl.BlockSpec((CONV_K, tc), lambda i, c: (0, jnp.minimum(step(i, c), n_conv - 1))),
            pl.BlockSpec((1, pg), lambda i, c: (0, group(i, c))),
            pl.BlockSpec((1, d), lambda i, c: (0, 0)),
            pl.BlockSpec((1, d), lambda i, c: (0, 0)),
        ],
        out_specs=pl.BlockSpec(memory_space=pl.ANY),
        out_shape=jax.ShapeDtypeStruct((m, d), F32),
        scratch_shapes=[
            pltpu.VMEM((ts, d), F32),
            pltpu.VMEM((2, HALO + ts, d), BF16),
            pltpu.VMEM((ts, pg), BF16),
            pltpu.SemaphoreType.DMA((OUT_COPIES_IN_FLIGHT,)),
        ],
        compiler_params=pltpu.CompilerParams(
            dimension_semantics=("arbitrary", "arbitrary"),
            vmem_limit_bytes=V7X_VMEM_LIMIT_BYTES,
        ),
        name="mixer_ln",
    )(h, h, h, meta_h, w_in, w_in, w_in, w_out, pool_w, conv_w, pool_scale, ln_g, ln_b)


def kernel(x, meta_tokens, ffn1_w_gu, ffn1_w_down, ln1_g, ln1_b, w_in, conv_w, pool_w, pool_scale,
           w_out, ln2_g, ln2_b, ffn2_w_gu, ffn2_w_down, ln3_g, ln3_b):
    b, seq, d = x.shape
    assert meta_tokens.shape == (N_META, d) and N_META == HALO
    assert ffn1_w_gu.shape[0] == DEPTH
    tm = 512

    h = x.reshape(b * seq, d)
    meta = meta_tokens.astype(x.dtype)
    for i in range(DEPTH):
        g1, b1 = ln1_g[i][None].astype(F32), ln1_b[i][None].astype(F32)
        g2, b2 = ln2_g[i][None].astype(F32), ln2_b[i][None].astype(F32)
        g3, b3 = ln3_g[i][None].astype(F32), ln3_b[i][None].astype(F32)
        n_groups, pg, _ = pool_w[i].shape

        h1_head, meta1, wgu1, wd1 = _ffn_ln_head(
            h, meta, ffn1_w_gu[i], ffn1_w_down[i], g1, b1, tm=tm)
        jobs = (
            _CastJob(ffn2_w_gu[i], (32, ffn2_w_gu[i].shape[1] // 2), 0, panel=V7X_MXU_COLS),
            _CastJob(ffn2_w_down[i], (64, d), 0),
            _CastJob(w_out[i], (64, d), ffn2_w_down[i].shape[0] // 64),
            _CastJob(w_in[i], (32, w_in[i].shape[1]), 0, panel=V7X_MXU_COLS),
            _CastJob(pool_w[i].reshape(n_groups * pg, pg), (128, pg), w_in[i].shape[0] // 32),
        )
        h1, (wgu2, wd2, w_out_b, w_in_b, pool_w_b) = _ffn_ln(
            h, wgu1, wd1, g1, b1, tm=768, tail=h1_head, cast_jobs=jobs)
        pool_w_b = pool_w_b.reshape(n_groups, pg, pg)

        h2 = _mixer_ln(h1, meta1, w_in_b, conv_w[i], pool_w_b, pool_scale[i][None], w_out_b, g2, b2,
                       ts=1024, seq=seq)
        h, _ = _ffn_ln(h2, wgu2, wd2, g3, b3, tm=1024)
        assert DEPTH == 1
    return h.reshape(b, seq, d)
```

```python
import functools
from typing import NamedTuple

import jax
import jax.numpy as jnp
from jax import lax
from jax.experimental import pallas as pl
from jax.experimental.pallas import tpu as pltpu

N_META = 16
CONV_K = 3
POOL_WINDOWS = (2, 4, 8, 16)
LN_EPS = 1e-5
DEPTH = 1
ALPHA = (2.0 * DEPTH) ** 0.25

BF16 = jnp.bfloat16
F32 = jnp.float32

V7X_MXU_COLS = 256
V7X_VMEM_LIMIT_BYTES = 60 * 1024 * 1024
HALO = 16
LN_ROWS = 8
LN_UNROLL = 16
FFN_DOWN_COLS = 512
X_SLAB_COLS = 128
MIXER_SLAB_COLS = 256
OUT_COPIES_IN_FLIGHT = 4


def _layer_norm_rows(y, g, b):
    mu = jnp.mean(y, axis=-1, keepdims=True)
    yc = y - mu
    var = jnp.mean(yc * yc, axis=-1, keepdims=True)
    return yc * lax.rsqrt(var + LN_EPS) * g + b


def _ln_group_rows(rows):
    return min(LN_ROWS * LN_UNROLL, rows)


def _layer_norm_group(o_ref, r0, res_ref, branch_scale, g_ref, b_ref):
    group = _ln_group_rows(o_ref.shape[0])
    outs = []
    for u in range(0, group, LN_ROWS):
        sub = pl.ds(r0 + u, LN_ROWS)
        y = o_ref[sub, :]
        if branch_scale is not None:
            y = branch_scale * y
        if res_ref is not None:
            y = ALPHA * res_ref[sub, :] + y
        outs.append(_layer_norm_rows(y, g_ref[...], b_ref[...]))
    for k, u in enumerate(range(0, group, LN_ROWS)):
        o_ref[pl.ds(r0 + u, LN_ROWS), :] = outs[k]


def _residual_layer_norm(o_ref, res_ref, branch_scale, g_ref, b_ref):
    rows = o_ref.shape[0]
    group = _ln_group_rows(rows)

    def body(r, carry):
        _layer_norm_group(o_ref, pl.multiple_of(r * group, group), res_ref, branch_scale, g_ref, b_ref)
        return carry

    lax.fori_loop(0, rows // group, body, 0)


def _swiglu_chunk(xb, wgu):
    tf = wgu.shape[1] // 2
    gu = jnp.dot(xb, wgu, preferred_element_type=F32)
    gate, up = gu[:, :tf], gu[:, tf:]
    return (gate * jax.nn.sigmoid(gate) * up).astype(BF16)


class _CastJob(NamedTuple):
    src: jax.Array
    block: tuple
    start: int
    panel: int = 0

    @property
    def n_blocks(self):
        return (self.src.shape[0] // self.block[0]) * (self.src.shape[1] // self.block[1])

    @property
    def out_shape(self):
        rows, cols = self.src.shape
        col_blocks = cols // self.block[1]
        shape = (self.block[1] // self.panel, rows, col_blocks * self.panel) if self.panel else (rows, cols)
        return jax.ShapeDtypeStruct(shape, BF16)

    def _block_index(self, steps_per_row_tile, i, j):
        col_blocks = self.src.shape[1] // self.block[1]
        s = jnp.clip(i * steps_per_row_tile + j - self.start, 0, self.n_blocks - 1)
        return s // col_blocks, s % col_blocks

    def in_spec(self, steps_per_row_tile):
        return pl.BlockSpec(self.block, functools.partial(self._block_index, steps_per_row_tile))

    def out_spec(self, steps_per_row_tile):
        if not self.panel:
            return self.in_spec(steps_per_row_tile)
        br, bc = self.block

        def index_map(i, j):
            rb, cb = self._block_index(steps_per_row_tile, i, j)
            return (0, rb, cb)

        return pl.BlockSpec((bc // self.panel, br, self.panel), index_map)


def _cast_block(src_ref, dst_ref):
    if len(dst_ref.shape) == 2:
        dst_ref[...] = src_ref[...].astype(BF16)
    else:
        panel = dst_ref.shape[2]
        for p in range(dst_ref.shape[0]):
            dst_ref[p] = src_ref[:, p * panel:(p + 1) * panel].astype(BF16)


def _ffn_kernel(*refs, n_tiles, tm, cast_ranges):
    n_cast = len(cast_ranges)
    n_fixed = 6
    xres_ref, xnext_ref, wgu_ref, wd_ref, g_ref, b_ref = refs[:n_fixed]
    cast_in = refs[n_fixed:n_fixed + n_cast]
    n_in = len(refs) - (6 + n_cast)
    tail_hbm = refs[n_fixed + n_cast] if n_in > n_fixed + n_cast else None
    out_hbm = refs[n_in]
    cast_out = refs[n_in + 1:n_in + 1 + n_cast]
    acc_ref, xb_ref, act_ref, sem, tail_sem = refs[n_in + 1 + n_cast:]
    i = pl.program_id(0)
    j = pl.program_id(1)
    nf = pl.num_programs(1)
    d = acc_ref.shape[1]
    n_slabs = d // X_SLAB_COLS
    slab = pl.ds(pl.multiple_of(jnp.minimum(j, n_slabs - 1) * X_SLAB_COLS, X_SLAB_COLS), X_SLAB_COLS)

    if tail_hbm is not None:
        tail_copy = pltpu.make_async_copy(
            tail_hbm, out_hbm.at[pl.ds(n_tiles * tm, tail_hbm.shape[0])], tail_sem)

        @pl.when(jnp.logical_and(i == 0, j == 0))
        def _():
            tail_copy.start()

        @pl.when(jnp.logical_and(i == n_tiles, j == nf - 1))
        def _():
            tail_copy.wait()

    @pl.when(jnp.logical_and(i < n_tiles, j < n_slabs))
    def _():
        xb_ref.at[i % 2][:, slab] = xnext_ref[...].astype(BF16)

    step = (i - 1) * nf + j
    for src_ref, dst_ref, (start, stop) in zip(cast_in, cast_out, cast_ranges):
        @pl.when(jnp.logical_and(step >= start, step < stop))
        def _(src_ref=src_ref, dst_ref=dst_ref):
            _cast_block(src_ref, dst_ref)

    @pl.when(i > 0)
    def _():
        @pl.when(j == 0)
        def _():
            acc_ref[...] = jnp.zeros_like(acc_ref)

        @pl.when(j < n_slabs)
        def _():
            acc_ref[:, slab] += (ALPHA / 0.5) * xres_ref[...]

        def down_project():
            act = act_ref[...]
            for n0 in range(0, d, FFN_DOWN_COLS):
                cols = slice(n0, n0 + FFN_DOWN_COLS)
                acc_ref[:, cols] += jnp.dot(act, wd_ref[:, cols], preferred_element_type=F32)

        def gate_up():
            act_ref[...] = _swiglu_chunk(xb_ref[(i - 1) % 2], wgu_ref[...])

        @pl.when(j == 0)
        def _():
            gate_up()

        @pl.when(jnp.logical_and(j > 0, j < nf - 1))
        def _():
            down_project()
            gate_up()

        @pl.when(j == nf - 1)
        def _():
            down_project()
            _layer_norm_and_write_back(acc_ref, out_hbm, sem, (i - 1) * tm, 0.5, g_ref, b_ref)


def _layer_norm_and_write_back(acc_ref, out_hbm, sem, row0, branch_scale, g_ref, b_ref):
    rows = acc_ref.shape[0]
    group = _ln_group_rows(rows)
    n_groups = rows // group
    in_flight = min(OUT_COPIES_IN_FLIGHT, n_groups)

    def group_copy(k):
        r0 = pl.multiple_of(k * group, group)
        return pltpu.make_async_copy(
            acc_ref.at[pl.ds(r0, group)], out_hbm.at[pl.ds(row0 + r0, group)], sem.at[k % in_flight])

    def body(k, carry):
        @pl.when(k >= in_flight)
        def _():
            group_copy(k - in_flight).wait()

        _layer_norm_group(acc_ref, pl.multiple_of(k * group, group), None, branch_scale, g_ref, b_ref)
        group_copy(k).start()
        return carry

    lax.fori_loop(0, n_groups, body, 0)
    for k in range(n_groups - in_flight, n_groups):
        group_copy(k).wait()


def _ffn_ln(x, wgu, w_down, ln_g, ln_b, *, tm, tail=None, cast_jobs=()):
    m, d = x.shape
    f = w_down.shape[0]
    tf = wgu.shape[2] // 2
    n_tiles = (m - (0 if tail is None else tail.shape[0])) // tm
    assert n_tiles * tm + (0 if tail is None else tail.shape[0]) == m
    assert f % tf == 0 and tf == V7X_MXU_COLS and tm % _ln_group_rows(tm) == 0
    nf = f // tf
    steps = nf + 1
    n_slabs = d // X_SLAB_COLS
    assert n_slabs <= steps
    for job in cast_jobs:
        assert job.start + job.n_blocks <= n_tiles * steps

    def gate_up_chunk(i, j):
        return jnp.where(i == 0, 0, jnp.minimum(j, nf - 1))

    def down_chunk(i, j):
        return jnp.where(i == 0, 0, jnp.maximum(j - 1, 0))

    def slab_col(j):
        return jnp.minimum(j, n_slabs - 1)

    shifted_jobs = [job._replace(start=job.start + steps) for job in cast_jobs]
    operands = [x, x, wgu, w_down, ln_g, ln_b] + [job.src for job in cast_jobs]
    if tail is not None:
        operands.append(tail)
    outs = pl.pallas_call(
        functools.partial(
            _ffn_kernel, n_tiles=n_tiles, tm=tm,
            cast_ranges=tuple((job.start, job.start + job.n_blocks) for job in cast_jobs)),
        grid=(n_tiles + 1, steps),
        in_specs=[
            pl.BlockSpec((tm, X_SLAB_COLS), lambda i, j: (jnp.maximum(i - 1, 0), slab_col(j))),
            pl.BlockSpec((tm, X_SLAB_COLS), lambda i, j: (jnp.minimum(i, n_tiles - 1), slab_col(j))),
            pl.BlockSpec((None, d, 2 * tf), lambda i, j: (gate_up_chunk(i, j), 0, 0)),
            pl.BlockSpec((tf, d), lambda i, j: (down_chunk(i, j), 0)),
            pl.BlockSpec((1, d), lambda i, j: (0, 0)),
            pl.BlockSpec((1, d), lambda i, j: (0, 0)),
        ] + [job.in_spec(steps) for job in shifted_jobs]
        + ([pl.BlockSpec(memory_space=pl.ANY)] if tail is not None else []),
        out_specs=[pl.BlockSpec(memory_space=pl.ANY)] + [job.out_spec(steps) for job in shifted_jobs],
        out_shape=[jax.ShapeDtypeStruct((m, d), F32)] + [job.out_shape for job in cast_jobs],
        scratch_shapes=[
            pltpu.VMEM((tm, d), F32),
            pltpu.VMEM((2, tm, d), BF16),
            pltpu.VMEM((tm, tf), BF16),
            pltpu.SemaphoreType.DMA((OUT_COPIES_IN_FLIGHT,)),
            pltpu.SemaphoreType.DMA(()),
        ],
        compiler_params=pltpu.CompilerParams(
            dimension_semantics=("arbitrary", "arbitrary"),
            vmem_limit_bytes=V7X_VMEM_LIMIT_BYTES,
        ),
        name="ffn_ln",
    )(*operands)
    return outs[0], tuple(outs[1:])


def _ffn_head_kernel(x_ref, meta_ref, wg_ref, wu_ref, wd_ref, g_ref, b_ref,
                     o_ref, om_ref, wgub_ref, wdb_ref, xb_ref):
    j = pl.program_id(0)
    n_meta = meta_ref.shape[0]

    @pl.when(j == 0)
    def _():
        xb_ref[0:n_meta, :] = meta_ref[...].astype(BF16)
        xb_ref[n_meta:, :] = x_ref[...].astype(BF16)
        o_ref[...] = jnp.zeros_like(o_ref)
        om_ref[...] = jnp.zeros_like(om_ref)

    tf = wg_ref.shape[1]
    wgub_ref[:, 0:tf] = wg_ref[...].astype(BF16)
    wgub_ref[:, tf:] = wu_ref[...].astype(BF16)
    wdb_ref[...] = wd_ref[...].astype(BF16)
    act = _swiglu_chunk(xb_ref[...], wgub_ref[...])
    d = o_ref.shape[1]
    for n0 in range(0, d, FFN_DOWN_COLS):
        cols = slice(n0, n0 + FFN_DOWN_COLS)
        part = jnp.dot(act, wdb_ref[:, cols], preferred_element_type=F32)
        om_ref[:, cols] += part[0:n_meta, :]
        o_ref[:, cols] += part[n_meta:, :]

    @pl.when(j == pl.num_programs(0) - 1)
    def _():
        _residual_layer_norm(om_ref, meta_ref, 0.5, g_ref, b_ref)
        _residual_layer_norm(o_ref, x_ref, 0.5, g_ref, b_ref)


def _ffn_ln_head(x, meta, w_gu, w_down, ln_g, ln_b, *, tm):
    m, d = x.shape
    n_meta = meta.shape[0]
    f = w_down.shape[0]
    tf = V7X_MXU_COLS
    nf = f // tf
    assert m % tm == 0
    last = m // tm - 1
    resident = pl.Buffered(1)
    return pl.pallas_call(
        _ffn_head_kernel,
        grid=(nf,),
        in_specs=[
            pl.BlockSpec((tm, d), lambda j: (last, 0), pipeline_mode=resident),
            pl.BlockSpec((n_meta, d), lambda j: (0, 0), pipeline_mode=resident),
            pl.BlockSpec((d, tf), lambda j: (0, j)),
            pl.BlockSpec((d, tf), lambda j: (0, nf + j)),
            pl.BlockSpec((tf, d), lambda j: (j, 0)),
            pl.BlockSpec((1, d), lambda j: (0, 0)),
            pl.BlockSpec((1, d), lambda j: (0, 0)),
        ],
        out_specs=[
            pl.BlockSpec((tm, d), lambda j: (0, 0), pipeline_mode=resident),
            pl.BlockSpec((n_meta, d), lambda j: (0, 0), pipeline_mode=resident),
            pl.BlockSpec((None, d, 2 * tf), lambda j: (j, 0, 0)),
            pl.BlockSpec((tf, d), lambda j: (j, 0)),
        ],
        out_shape=[
            jax.ShapeDtypeStruct((tm, d), F32),
            jax.ShapeDtypeStruct((n_meta, d), F32),
            jax.ShapeDtypeStruct((nf, d, 2 * tf), BF16),
            jax.ShapeDtypeStruct((f, d), BF16),
        ],
        scratch_shapes=[pltpu.VMEM((n_meta + tm, d), BF16)],
        compiler_params=pltpu.CompilerParams(
            dimension_semantics=("arbitrary",),
            vmem_limit_bytes=V7X_VMEM_LIMIT_BYTES,
        ),
        name="ffn_ln_head",
    )(x, meta, w_gu, w_gu, w_down, ln_g, ln_b)


def _mixer_kernel(hres_ref, hnext_ref, halo_ref, meta_ref, wa_ref, wb_ref, wc_ref, wo_ref, pw_ref,
                  cw_ref, ps_ref, g_ref, b_ref, out_hbm, acc_ref, xb_ref, yp_ref, sem,
                  *, n_tiles, ts, tiles_per_seq, n_conv):
    i = pl.program_id(0)
    c = pl.program_id(1)
    d = acc_ref.shape[1]
    tc = wa_ref.shape[1]
    n_slabs = d // MIXER_SLAB_COLS
    slab = pl.ds(
        pl.multiple_of(jnp.minimum(c, n_slabs - 1) * MIXER_SLAB_COLS, MIXER_SLAB_COLS), MIXER_SLAB_COLS)

    def round_next_slab():
        first = (i % tiles_per_seq) == 0
        nxt = xb_ref.at[i % 2]
        nxt[0:HALO, slab] = jnp.where(first, meta_ref[...], halo_ref[...]).astype(BF16)
        nxt[HALO:, slab] = hnext_ref[...].astype(BF16)

    def accumulate(y):
        for n0 in range(0, d, FFN_DOWN_COLS):
            cols = slice(n0, n0 + FFN_DOWN_COLS)
            acc_ref[:, cols] += jnp.dot(y, wo_ref[:, cols], preferred_element_type=F32)

    @pl.when(jnp.logical_and(i < n_tiles, c < n_slabs))
    def _():
        round_next_slab()

    @pl.when(i > 0)
    def _():
        @pl.when(c == 0)
        def _():
            acc_ref[...] = jnp.zeros_like(acc_ref)

        @pl.when(c < n_slabs)
        def _():
            acc_ref[:, slab] += ALPHA * hres_ref[...]

        pool_step = c - n_conv
        group = pool_step // 2

        @pl.when(c < n_conv)
        def _():
            half = HALO + (ts // 2)
            hb_ref = xb_ref.at[(i - 1) % 2]
            tops = [jnp.dot(hb_ref[0:half, :], w[...], preferred_element_type=F32)
                    for w in (wa_ref, wb_ref, wc_ref)]
            bots = [jnp.dot(hb_ref[half:, :], w[...], preferred_element_type=F32)
                    for w in (wa_ref, wb_ref, wc_ref)]
            gate_b, gate_c, x_in = [jnp.concatenate(tb, axis=0) for tb in zip(tops, bots)]
            v = gate_c * x_in
            conv = pltpu.roll(v, 2, 0) * cw_ref[0:1, :]
            conv = conv + pltpu.roll(v, 1, 0) * cw_ref[1:2, :]
            conv = conv + v * cw_ref[2:3, :]
            accumulate((gate_b * conv)[HALO:, :].astype(BF16))

        @pl.when(jnp.logical_and(pool_step >= 0, pool_step % 2 == 0))
        def _():
            hb = xb_ref[(i - 1) % 2]
            z = jnp.concatenate([jnp.dot(hb, wa_ref[...], preferred_element_type=F32),
                                 jnp.dot(hb, wb_ref[...], preferred_element_type=F32)], axis=1)
            s = z
            for level in range(len(POOL_WINDOWS)):
                doubled = s + pltpu.roll(s, 1 << level, 0)
                s = doubled if level == 0 else jnp.where(level <= group, doubled, s)
            inv_w = jnp.where(group == 0, 1.0 / POOL_WINDOWS[0],
                              jnp.where(group == 1, 1.0 / POOL_WINDOWS[1],
                                        jnp.where(group == 2, 1.0 / POOL_WINDOWS[2],
                                                  1.0 / POOL_WINDOWS[3])))
            diff = (s * inv_w.astype(F32) - z)[HALO:, :].astype(BF16)
            y = (jnp.dot(diff, pw_ref[...], preferred_element_type=F32) * ps_ref[...]).astype(BF16)
            yp_ref[...] = y
            accumulate(y[:, 0:tc])

        @pl.when(jnp.logical_and(pool_step >= 0, pool_step % 2 == 1))
        def _():
            accumulate(yp_ref[:, tc:])

        @pl.when(c == pl.num_programs(1) - 1)
        def _():
            _layer_norm_and_write_back(acc_ref, out_hbm, sem, (i - 1) * ts, None, g_ref, b_ref)


def _mixer_ln(h, meta_h, w_in, conv_w, pool_w, pool_scale, w_out, ln_g, ln_b, *, ts, seq):
    m, d = h.shape
    cc = conv_w.shape[1]
    n_groups, pg, _ = pool_w.shape
    n_panels, _, tc = w_in.shape
    n_conv = cc // tc
    assert pg == 2 * tc and n_panels == 3 * n_conv + 2 * n_groups and w_out.shape[0] == cc + n_groups * pg
    n_steps = n_conv + 2 * n_groups
    n_tiles = m // ts
    tiles_per_seq = seq // ts
    n_slabs = d // MIXER_SLAB_COLS
    assert m % ts == 0 and seq % ts == 0 and n_slabs <= n_steps and ts % _ln_group_rows(ts) == 0

    def step(i, c):
        return jnp.where(i == 0, 0, c)

    def group(i, c):
        return jnp.maximum(step(i, c) - n_conv, 0) // 2

    def slab_col(c):
        return jnp.minimum(c, n_slabs - 1)

    def next_tile(i):
        return jnp.minimum(i, n_tiles - 1)

    def pool_or(i, c, conv_panel, pool_panel):
        return jnp.where(step(i, c) < n_conv, conv_panel, pool_panel)

    return pl.pallas_call(
        functools.partial(
            _mixer_kernel, n_tiles=n_tiles, ts=ts, tiles_per_seq=tiles_per_seq, n_conv=n_conv),
        grid=(n_tiles + 1, n_steps),
        in_specs=[
            pl.BlockSpec((ts, MIXER_SLAB_COLS), lambda i, c: (jnp.maximum(i - 1, 0), slab_col(c))),
            pl.BlockSpec((ts, MIXER_SLAB_COLS), lambda i, c: (next_tile(i), slab_col(c))),
            pl.BlockSpec((HALO, MIXER_SLAB_COLS),
                         lambda i, c: (jnp.maximum(next_tile(i) * (ts // HALO) - 1, 0), slab_col(c))),
            pl.BlockSpec((HALO, MIXER_SLAB_COLS), lambda i, c: (0, slab_col(c))),
            pl.BlockSpec((None, d, tc), lambda i, c: (
                pool_or(i, c, step(i, c), 3 * n_conv + 2 * group(i, c)), 0, 0)),
            pl.BlockSpec((None, d, tc), lambda i, c: (
                pool_or(i, c, n_conv + step(i, c), 3 * n_conv + 2 * group(i, c) + 1), 0, 0)),
            pl.BlockSpec((None, d, tc), lambda i, c: (
                2 * n_conv + jnp.minimum(step(i, c), n_conv - 1), 0, 0)),
            pl.BlockSpec((tc, d), lambda i, c: (step(i, c), 0)),
            pl.BlockSpec((None, pg, pg), lambda i, c: (group(i, c), 0, 0)),
            pl.BlockSpec((CONV_K, tc), lambda i, c: (0, jnp.minimum(step(i, c), n_conv - 1))),
            pl.BlockSpec((1, pg), lambda i, c: (0, group(i, c))),
            pl.BlockSpec((1, d), lambda i, c: (0, 0)),
            pl.BlockSpec((1, d), lambda i, c: (0, 0)),
        ],
        out_specs=pl.BlockSpec(memory_space=pl.ANY),
        out_shape=jax.ShapeDtypeStruct((m, d), F32),
        scratch_shapes=[
            pltpu.VMEM((ts, d), F32),
            pltpu.VMEM((2, HALO + ts, d), BF16),
            pltpu.VMEM((ts, pg), BF16),
            pltpu.SemaphoreType.DMA((OUT_COPIES_IN_FLIGHT,)),
        ],
        compiler_params=pltpu.CompilerParams(
            dimension_semantics=("arbitrary", "arbitrary"),
            vmem_limit_bytes=V7X_VMEM_LIMIT_BYTES,
        ),
        name="mixer_ln",
    )(h, h, h, meta_h, w_in, w_in, w_in, w_out, pool_w, conv_w, pool_scale, ln_g, ln_b)


def kernel(x, meta_tokens, ffn1_w_gu, ffn1_w_down, ln1_g, ln1_b, w_in, conv_w, pool_w, pool_scale,
           w_out, ln2_g, ln2_b, ffn2_w_gu, ffn2_w_down, ln3_g, ln3_b):
    b, seq, d = x.shape
    assert meta_tokens.shape == (N_META, d) and N_META == HALO
    assert ffn1_w_gu.shape[0] == DEPTH
    tm = 512

    h = x.reshape(b * seq, d)
    meta = meta_tokens.astype(x.dtype)
    for i in range(DEPTH):
        g1, b1 = ln1_g[i][None].astype(F32), ln1_b[i][None].astype(F32)
        g2, b2 = ln2_g[i][None].astype(F32), ln2_b[i][None].astype(F32)
        g3, b3 = ln3_g[i][None].astype(F32), ln3_b[i][None].astype(F32)
        n_groups, pg, _ = pool_w[i].shape

        h1_head, meta1, wgu1, wd1 = _ffn_ln_head(
            h, meta, ffn1_w_gu[i], ffn1_w_down[i], g1, b1, tm=tm)
        jobs = (
            _CastJob(ffn2_w_gu[i], (32, ffn2_w_gu[i].shape[1] // 2), 0, panel=V7X_MXU_COLS),
            _CastJob(ffn2_w_down[i], (64, d), 0),
            _CastJob(w_out[i], (64, d), ffn2_w_down[i].shape[0] // 64),
            _CastJob(w_in[i], (32, w_in[i].shape[1]), 0, panel=V7X_MXU_COLS),
            _CastJob(pool_w[i].reshape(n_groups * pg, pg), (128, pg), w_in[i].shape[0] // 32),
        )
        h1, (wgu2, wd2, w_out_b, w_in_b, pool_w_b) = _ffn_ln(
            h, wgu1, wd1, g1, b1, tm=768, tail=h1_head, cast_jobs=jobs)
        pool_w_b = pool_w_b.reshape(n_groups, pg, pg)

        h2 = _mixer_ln(h1, meta1, w_in_b, conv_w[i], pool_w_b, pool_scale[i][None], w_out_b, g2, b2,
                       ts=1024, seq=seq)
        h, _ = _ffn_ln(h2, wgu2, wd2, g3, b3, tm=1024)
        assert DEPTH == 1
    return h.reshape(b, seq, d)
```

```python
import functools
from typing import NamedTuple

import jax
import jax.numpy as jnp
from jax import lax
from jax.experimental import pallas as pl
from jax.experimental.pallas import tpu as pltpu

N_META = 16
CONV_K = 3
POOL_WINDOWS = (2, 4, 8, 16)
LN_EPS = 1e-5
DEPTH = 1
ALPHA = (2.0 * DEPTH) ** 0.25

BF16 = jnp.bfloat16
F32 = jnp.float32

V7X_MXU_COLS = 256
V7X_VMEM_LIMIT_BYTES = 60 * 1024 * 1024
HALO = 16
LN_ROWS = 8
LN_UNROLL = 16
FFN_DOWN_COLS = 512
X_SLAB_COLS = 128
MIXER_SLAB_COLS = 256
OUT_COPIES_IN_FLIGHT = 4


def _layer_norm_rows(y, g, b):
    mu = jnp.mean(y, axis=-1, keepdims=True)
    yc = y - mu
    var = jnp.mean(yc * yc, axis=-1, keepdims=True)
    return yc * lax.rsqrt(var + LN_EPS) * g + b


def _ln_group_rows(rows):
    return min(LN_ROWS * LN_UNROLL, rows)


def _layer_norm_group(o_ref, r0, res_ref, branch_scale, g_ref, b_ref):
    group = _ln_group_rows(o_ref.shape[0])
    outs = []
    for u in range(0, group, LN_ROWS):
        sub = pl.ds(r0 + u, LN_ROWS)
        y = o_ref[sub, :]
        if branch_scale is not None:
            y = branch_scale * y
        if res_ref is not None:
            y = ALPHA * res_ref[sub, :] + y
        outs.append(_layer_norm_rows(y, g_ref[...], b_ref[...]))
    for k, u in enumerate(range(0, group, LN_ROWS)):
        o_ref[pl.ds(r0 + u, LN_ROWS), :] = outs[k]


def _residual_layer_norm(o_ref, res_ref, branch_scale, g_ref, b_ref):
    rows = o_ref.shape[0]
    group = _ln_group_rows(rows)

    def body(r, carry):
        _layer_norm_group(o_ref, pl.multiple_of(r * group, group), res_ref, branch_scale, g_ref, b_ref)
        return carry

    lax.fori_loop(0, rows // group, body, 0)


def _swiglu_chunk(xb, wgu):
    tf = wgu.shape[1] // 2
    gu = jnp.dot(xb, wgu, preferred_element_type=F32)
    gate, up = gu[:, :tf], gu[:, tf:]
    return (gate * jax.nn.sigmoid(gate) * up).astype(BF16)


class _CastJob(NamedTuple):
    src: jax.Array
    block: tuple
    start: int
    panel: int = 0

    @property
    def n_blocks(self):
        return (self.src.shape[0] // self.block[0]) * (self.src.shape[1] // self.block[1])

    @property
    def out_shape(self):
        rows, cols = self.src.shape
        col_blocks = cols // self.block[1]
        shape = (self.block[1] // self.panel, rows, col_blocks * self.panel) if self.panel else (rows, cols)
        return jax.ShapeDtypeStruct(shape, BF16)

    def _block_index(self, steps_per_row_tile, i, j):
        col_blocks = self.src.shape[1] // self.block[1]
        s = jnp.clip(i * steps_per_row_tile + j - self.start, 0, self.n_blocks - 1)
        return s // col_blocks, s % col_blocks

    def in_spec(self, steps_per_row_tile):
        return pl.BlockSpec(self.block, functools.partial(self._block_index, steps_per_row_tile))

    def out_spec(self, steps_per_row_tile):
        if not self.panel:
            return self.in_spec(steps_per_row_tile)
        br, bc = self.block

        def index_map(i, j):
            rb, cb = self._block_index(steps_per_row_tile, i, j)
            return (0, rb, cb)

        return pl.BlockSpec((bc // self.panel, br, self.panel), index_map)


def _cast_block(src_ref, dst_ref):
    if len(dst_ref.shape) == 2:
        dst_ref[...] = src_ref[...].astype(BF16)
    else:
        panel = dst_ref.shape[2]
        for p in range(dst_ref.shape[0]):
            dst_ref[p] = src_ref[:, p * panel:(p + 1) * panel].astype(BF16)


def _ffn_kernel(*refs, n_tiles, tm, cast_ranges):
    n_cast = len(cast_ranges)
    n_fixed = 6
    xres_ref, xnext_ref, wgu_ref, wd_ref, g_ref, b_ref = refs[:n_fixed]
    cast_in = refs[n_fixed:n_fixed + n_cast]
    n_in = len(refs) - (6 + n_cast)
    tail_hbm = refs[n_fixed + n_cast] if n_in > n_fixed + n_cast else None
    out_hbm = refs[n_in]
    cast_out = refs[n_in + 1:n_in + 1 + n_cast]
    acc_ref, xb_ref, act_ref, sem, tail_sem = refs[n_in + 1 + n_cast:]
    i = pl.program_id(0)
    j = pl.program_id(1)
    nf = pl.num_programs(1)
    d = acc_ref.shape[1]
    n_slabs = d // X_SLAB_COLS
    slab = pl.ds(pl.multiple_of(jnp.minimum(j, n_slabs - 1) * X_SLAB_COLS, X_SLAB_COLS), X_SLAB_COLS)

    if tail_hbm is not None:
        tail_copy = pltpu.make_async_copy(
            tail_hbm, out_hbm.at[pl.ds(n_tiles * tm, tail_hbm.shape[0])], tail_sem)

        @pl.when(jnp.logical_and(i == 0, j == 0))
        def _():
            tail_copy.start()

        @pl.when(jnp.logical_and(i == n_tiles, j == nf - 1))
        def _():
            tail_copy.wait()

    @pl.when(jnp.logical_and(i < n_tiles, j < n_slabs))
    def _():
        xb_ref.at[i % 2][:, slab] = xnext_ref[...].astype(BF16)

    step = (i - 1) * nf + j
    for src_ref, dst_ref, (start, stop) in zip(cast_in, cast_out, cast_ranges):
        @pl.when(jnp.logical_and(step >= start, step < stop))
        def _(src_ref=src_ref, dst_ref=dst_ref):
            _cast_block(src_ref, dst_ref)

    @pl.when(i > 0)
    def _():
        @pl.when(j == 0)
        def _():
            acc_ref[...] = jnp.zeros_like(acc_ref)

        @pl.when(j < n_slabs)
        def _():
            acc_ref[:, slab] += (ALPHA / 0.5) * xres_ref[...]

        tf = wd_ref.shape[0] // 2

        def down_project(k):
            act = act_ref[:, 0:k]
            for n0 in range(0, d, FFN_DOWN_COLS):
                cols = slice(n0, n0 + FFN_DOWN_COLS)
                acc_ref[:, cols] += jnp.dot(act, wd_ref[0:k, cols], preferred_element_type=F32)

        def gate_up():
            half = pl.ds(pl.multiple_of((j % 2) * tf, tf), tf)
            act_ref[:, half] = _swiglu_chunk(xb_ref[(i - 1) % 2], wgu_ref[...])

        pair_ready = jnp.logical_and(j % 2 == 0, j >= 2)

        @pl.when(jnp.logical_and(jnp.logical_not(pair_ready), j < nf - 1))
        def _():
            gate_up()

        @pl.when(jnp.logical_and(pair_ready, j < nf - 1))
        def _():
            down_project(2 * tf)
            gate_up()

        @pl.when(j == nf - 1)
        def _():
            down_project(tf)
            _layer_norm_and_write_back(acc_ref, out_hbm, sem, (i - 1) * tm, 0.5, g_ref, b_ref)


def _layer_norm_and_write_back(acc_ref, out_hbm, sem, row0, branch_scale, g_ref, b_ref):
    rows = acc_ref.shape[0]
    group = _ln_group_rows(rows)
    n_groups = rows // group
    in_flight = min(OUT_COPIES_IN_FLIGHT, n_groups)

    def group_copy(k):
        r0 = pl.multiple_of(k * group, group)
        return pltpu.make_async_copy(
            acc_ref.at[pl.ds(r0, group)], out_hbm.at[pl.ds(row0 + r0, group)], sem.at[k % in_flight])

    def body(k, carry):
        @pl.when(k >= in_flight)
        def _():
            group_copy(k - in_flight).wait()

        _layer_norm_group(acc_ref, pl.multiple_of(k * group, group), None, branch_scale, g_ref, b_ref)
        group_copy(k).start()
        return carry

    lax.fori_loop(0, n_groups, body, 0)
    for k in range(n_groups - in_flight, n_groups):
        group_copy(k).wait()


def _ffn_ln(x, wgu, w_down, ln_g, ln_b, *, tm, tail=None, cast_jobs=()):
    m, d = x.shape
    f = w_down.shape[0]
    tf = wgu.shape[2] // 2
    n_tiles = (m - (0 if tail is None else tail.shape[0])) // tm
    assert n_tiles * tm + (0 if tail is None else tail.shape[0]) == m
    assert f % tf == 0 and tf == V7X_MXU_COLS and tm % _ln_group_rows(tm) == 0
    nf = f // tf
    steps = nf + 1
    n_slabs = d // X_SLAB_COLS
    assert n_slabs <= steps
    for job in cast_jobs:
        assert job.start + job.n_blocks <= n_tiles * steps

    def gate_up_chunk(i, j):
        return jnp.where(i == 0, 0, jnp.minimum(j, nf - 1))

    assert nf % 2 == 1

    def down_pair(i, j):
        pair = jnp.where(j == steps - 1, nf // 2, jnp.maximum(j - 2, 0) // 2)
        return jnp.where(i == 0, 0, pair)

    def slab_col(j):
        return jnp.minimum(j, n_slabs - 1)

    shifted_jobs = [job._replace(start=job.start + steps) for job in cast_jobs]
    operands = [x, x, wgu, w_down, ln_g, ln_b] + [job.src for job in cast_jobs]
    if tail is not None:
        operands.append(tail)
    outs = pl.pallas_call(
        functools.partial(
            _ffn_kernel, n_tiles=n_tiles, tm=tm,
            cast_ranges=tuple((job.start, job.start + job.n_blocks) for job in cast_jobs)),
        grid=(n_tiles + 1, steps),
        in_specs=[
            pl.BlockSpec((tm, X_SLAB_COLS), lambda i, j: (jnp.maximum(i - 1, 0), slab_col(j))),
            pl.BlockSpec((tm, X_SLAB_COLS), lambda i, j: (jnp.minimum(i, n_tiles - 1), slab_col(j))),
            pl.BlockSpec((None, d, 2 * tf), lambda i, j: (gate_up_chunk(i, j), 0, 0)),
            pl.BlockSpec((2 * tf, d), lambda i, j: (down_pair(i, j), 0)),
            pl.BlockSpec((1, d), lambda i, j: (0, 0)),
            pl.BlockSpec((1, d), lambda i, j: (0, 0)),
        ] + [job.in_spec(steps) for job in shifted_jobs]
        + ([pl.BlockSpec(memory_space=pl.ANY)] if tail is not None else []),
        out_specs=[pl.BlockSpec(memory_space=pl.ANY)] + [job.out_spec(steps) for job in shifted_jobs],
        out_shape=[jax.ShapeDtypeStruct((m, d), F32)] + [job.out_shape for job in cast_jobs],
        scratch_shapes=[
            pltpu.VMEM((tm, d), F32),
            pltpu.VMEM((2, tm, d), BF16),
            pltpu.VMEM((tm, 2 * tf), BF16),
            pltpu.SemaphoreType.DMA((OUT_COPIES_IN_FLIGHT,)),
            pltpu.SemaphoreType.DMA(()),
        ],
        compiler_params=pltpu.CompilerParams(
            dimension_semantics=("arbitrary", "arbitrary"),
            vmem_limit_bytes=V7X_VMEM_LIMIT_BYTES,
        ),
        name="ffn_ln",
    )(*operands)
    return outs[0], tuple(outs[1:])


def _ffn_head_kernel(x_ref, meta_ref, wg_ref, wu_ref, wd_ref, g_ref, b_ref,
                     o_ref, om_ref, wgub_ref, wdb_ref, xb_ref):
    j = pl.program_id(0)
    n_meta = meta_ref.shape[0]

    @pl.when(j == 0)
    def _():
        xb_ref[0:n_meta, :] = meta_ref[...].astype(BF16)
        xb_ref[n_meta:, :] = x_ref[...].astype(BF16)
        o_ref[...] = jnp.zeros_like(o_ref)
        om_ref[...] = jnp.zeros_like(om_ref)

    tf = wg_ref.shape[1]
    wgub_ref[:, 0:tf] = wg_ref[...].astype(BF16)
    wgub_ref[:, tf:] = wu_ref[...].astype(BF16)
    wdb_ref[...] = wd_ref[...].astype(BF16)
    act = _swiglu_chunk(xb_ref[...], wgub_ref[...])
    d = o_ref.shape[1]
    for n0 in range(0, d, FFN_DOWN_COLS):
        cols = slice(n0, n0 + FFN_DOWN_COLS)
        part = jnp.dot(act, wdb_ref[:, cols], preferred_element_type=F32)
        om_ref[:, cols] += part[0:n_meta, :]
        o_ref[:, cols] += part[n_meta:, :]

    @pl.when(j == pl.num_programs(0) - 1)
    def _():
        _residual_layer_norm(om_ref, meta_ref, 0.5, g_ref, b_ref)
        _residual_layer_norm(o_ref, x_ref, 0.5, g_ref, b_ref)


def _ffn_ln_head(x, meta, w_gu, w_down, ln_g, ln_b, *, tm):
    m, d = x.shape
    n_meta = meta.shape[0]
    f = w_down.shape[0]
    tf = V7X_MXU_COLS
    nf = f // tf
    assert m % tm == 0
    last = m // tm - 1
    resident = pl.Buffered(1)
    return pl.pallas_call(
        _ffn_head_kernel,
        grid=(nf,),
        in_specs=[
            pl.BlockSpec((tm, d), lambda j: (last, 0), pipeline_mode=resident),
            pl.BlockSpec((n_meta, d), lambda j: (0, 0), pipeline_mode=resident),
            pl.BlockSpec((d, tf), lambda j: (0, j)),
            pl.BlockSpec((d, tf), lambda j: (0, nf + j)),
            pl.BlockSpec((tf, d), lambda j: (j, 0)),
            pl.BlockSpec((1, d), lambda j: (0, 0)),
            pl.BlockSpec((1, d), lambda j: (0, 0)),
        ],
        out_specs=[
            pl.BlockSpec((tm, d), lambda j: (0, 0), pipeline_mode=resident),
            pl.BlockSpec((n_meta, d), lambda j: (0, 0), pipeline_mode=resident),
            pl.BlockSpec((None, d, 2 * tf), lambda j: (j, 0, 0)),
            pl.BlockSpec((tf, d), lambda j: (j, 0)),
        ],
        out_shape=[
            jax.ShapeDtypeStruct((tm, d), F32),
            jax.ShapeDtypeStruct((n_meta, d), F32),
            jax.ShapeDtypeStruct((nf, d, 2 * tf), BF16),
            jax.ShapeDtypeStruct((f, d), BF16),
        ],
        scratch_shapes=[pltpu.VMEM((n_meta + tm, d), BF16)],
        compiler_params=pltpu.CompilerParams(
            dimension_semantics=("arbitrary",),
            vmem_limit_bytes=V7X_VMEM_LIMIT_BYTES,
        ),
        name="ffn_ln_head",
    )(x, meta, w_gu, w_gu, w_down, ln_g, ln_b)


def _mixer_kernel(hres_ref, hnext_ref, halo_ref, meta_ref, wa_ref, wb_ref, wc_ref, wo_ref, pw_ref,
                  cw_ref, ps_ref, g_ref, b_ref, out_hbm, acc_ref, xb_ref, yp_ref, sem,
                  *, n_tiles, ts, tiles_per_seq, n_conv):
    i = pl.program_id(0)
    c = pl.program_id(1)
    d = acc_ref.shape[1]
    tc = wa_ref.shape[1]
    n_slabs = d // MIXER_SLAB_COLS
    slab = pl.ds(
        pl.multiple_of(jnp.minimum(c, n_slabs - 1) * MIXER_SLAB_COLS, MIXER_SLAB_COLS), MIXER_SLAB_COLS)

    def round_next_slab():
        first = (i % tiles_per_seq) == 0
        nxt = xb_ref.at[i % 2]
        nxt[0:HALO, slab] = jnp.where(first, meta_ref[...], halo_ref[...]).astype(BF16)
        nxt[HALO:, slab] = hnext_ref[...].astype(BF16)

    def accumulate(y):
        for n0 in range(0, d, FFN_DOWN_COLS):
            cols = slice(n0, n0 + FFN_DOWN_COLS)
            acc_ref[:, cols] += jnp.dot(y, wo_ref[:, cols], preferred_element_type=F32)

    @pl.when(jnp.logical_and(i < n_tiles, c < n_slabs))
    def _():
        round_next_slab()

    @pl.when(i > 0)
    def _():
        @pl.when(c == 0)
        def _():
            acc_ref[...] = jnp.zeros_like(acc_ref)

        @pl.when(c < n_slabs)
        def _():
            acc_ref[:, slab] += ALPHA * hres_ref[...]

        pool_step = c - n_conv
        group = pool_step // 2

        @pl.when(c < n_conv)
        def _():
            half = HALO + (ts // 2)
            hb_ref = xb_ref.at[(i - 1) % 2]
            tops = [jnp.dot(hb_ref[0:half, :], w[...], preferred_element_type=F32)
                    for w in (wa_ref, wb_ref, wc_ref)]
            bots = [jnp.dot(hb_ref[half:, :], w[...], preferred_element_type=F32)
                    for w in (wa_ref, wb_ref, wc_ref)]
            gate_b, gate_c, x_in = [jnp.concatenate(tb, axis=0) for tb in zip(tops, bots)]
            v = gate_c * x_in
            conv = pltpu.roll(v, 2, 0) * cw_ref[0:1, :]
            conv = conv + pltpu.roll(v, 1, 0) * cw_ref[1:2, :]
            conv = conv + v * cw_ref[2:3, :]
            accumulate((gate_b * conv)[HALO:, :].astype(BF16))

        @pl.when(jnp.logical_and(pool_step >= 0, pool_step % 2 == 0))
        def _():
            hb = xb_ref[(i - 1) % 2]
            z = jnp.concatenate([jnp.dot(hb, wa_ref[...], preferred_element_type=F32),
                                 jnp.dot(hb, wb_ref[...], preferred_element_type=F32)], axis=1)
            s = z
            for level in range(len(POOL_WINDOWS)):
                doubled = s + pltpu.roll(s, 1 << level, 0)
                s = doubled if level == 0 else jnp.where(level <= group, doubled, s)
            inv_w = jnp.where(group == 0, 1.0 / POOL_WINDOWS[0],
                              jnp.where(group == 1, 1.0 / POOL_WINDOWS[1],
                                        jnp.where(group == 2, 1.0 / POOL_WINDOWS[2],
                                                  1.0 / POOL_WINDOWS[3])))
            diff = (s * inv_w.astype(F32) - z)[HALO:, :].astype(BF16)
            y = (jnp.dot(diff, pw_ref[...], preferred_element_type=F32) * ps_ref[...]).astype(BF16)
            yp_ref[...] = y
            accumulate(y[:, 0:tc])

        @pl.when(jnp.logical_and(pool_step >= 0, pool_step % 2 == 1))
        def _():
            accumulate(yp_ref[:, tc:])

        @pl.when(c == pl.num_programs(1) - 1)
        def _():
            _layer_norm_and_write_back(acc_ref, out_hbm, sem, (i - 1) * ts, None, g_ref, b_ref)


def _mixer_ln(h, meta_h, w_in, conv_w, pool_w, pool_scale, w_out, ln_g, ln_b, *, ts, seq):
    m, d = h.shape
    cc = conv_w.shape[1]
    n_groups, pg, _ = pool_w.shape
    n_panels, _, tc = w_in.shape
    n_conv = cc // tc
    assert pg == 2 * tc and n_panels == 3 * n_conv + 2 * n_groups and w_out.shape[0] == cc + n_groups * pg
    n_steps = n_conv + 2 * n_groups
    n_tiles = m // ts
    tiles_per_seq = seq // ts
    n_slabs = d // MIXER_SLAB_COLS
    assert m % ts == 0 and seq % ts == 0 and n_slabs <= n_steps and ts % _ln_group_rows(ts) == 0

    def step(i, c):
        return jnp.where(i == 0, 0, c)

    def group(i, c):
        return jnp.maximum(step(i, c) - n_conv, 0) // 2

    def slab_col(c):
        return jnp.minimum(c, n_slabs - 1)

    def next_tile(i):
        return jnp.minimum(i, n_tiles - 1)

    def pool_or(i, c, conv_panel, pool_panel):
        return jnp.where(step(i, c) < n_conv, conv_panel, pool_panel)

    return pl.pallas_call(
        functools.partial(
            _mixer_kernel, n_tiles=n_tiles, ts=ts, tiles_per_seq=tiles_per_seq, n_conv=n_conv),
        grid=(n_tiles + 1, n_steps),
        in_specs=[
            pl.BlockSpec((ts, MIXER_SLAB_COLS), lambda i, c: (jnp.maximum(i - 1, 0), slab_col(c))),
            pl.BlockSpec((ts, MIXER_SLAB_COLS), lambda i, c: (next_tile(i), slab_col(c))),
            pl.BlockSpec((HALO, MIXER_SLAB_COLS),
                         lambda i, c: (jnp.maximum(next_tile(i) * (ts // HALO) - 1, 0), slab_col(c))),
            pl.BlockSpec((HALO, MIXER_SLAB_COLS), lambda i, c: (0, slab_col(c))),
            pl.BlockSpec((None, d, tc), lambda i, c: (
                pool_or(i, c, step(i, c), 3 * n_conv + 2 * group(i, c)), 0, 0)),
            pl.BlockSpec((None, d, tc), lambda i, c: (
                pool_or(i, c, n_conv + step(i, c), 3 * n_conv + 2 * group(i, c) + 1), 0, 0)),
            pl.BlockSpec((None, d, tc), lambda i, c: (
                2 * n_conv + jnp.minimum(step(i, c), n_conv - 1), 0, 0)),
            pl.BlockSpec((tc, d), lambda i, c: (step(i, c), 0)),
            pl.BlockSpec((None, pg, pg), lambda i, c: (group(i, c), 0, 0)),
            pl.BlockSpec((CONV_K, tc), lambda i, c: (0, jnp.minimum(step(i, c), n_conv - 1))),
            pl.BlockSpec((1, pg), lambda i, c: (0, group(i, c))),
            pl.BlockSpec((1, d), lambda i, c: (0, 0)),
            pl.BlockSpec((1, d), lambda i, c: (0, 0)),
        ],
        out_specs=pl.BlockSpec(memory_space=pl.ANY),
        out_shape=jax.ShapeDtypeStruct((m, d), F32),
        scratch_shapes=[
            pltpu.VMEM((ts, d), F32),
            pltpu.VMEM((2, HALO + ts, d), BF16),
            pltpu.VMEM((ts, pg), BF16),
            pltpu.SemaphoreType.DMA((OUT_COPIES_IN_FLIGHT,)),
        ],
        compiler_params=pltpu.CompilerParams(
            dimension_semantics=("arbitrary", "arbitrary"),
            vmem_limit_bytes=V7X_VMEM_LIMIT_BYTES,
        ),
        name="mixer_ln",
    )(h, h, h, meta_h, w_in, w_in, w_in, w_out, pool_w, conv_w, pool_scale, ln_g, ln_b)


def kernel(x, meta_tokens, ffn1_w_gu, ffn1_w_down, ln1_g, ln1_b, w_in, conv_w, pool_w, pool_scale,
           w_out, ln2_g, ln2_b, ffn2_w_gu, ffn2_w_down, ln3_g, ln3_b):
    b, seq, d = x.shape
    assert meta_tokens.shape == (N_META, d) and N_META == HALO
    assert ffn1_w_gu.shape[0] == DEPTH
    tm = 512

    h = x.reshape(b * seq, d)
    meta = meta_tokens.astype(x.dtype)
    for i in range(DEPTH):
        g1, b1 = ln1_g[i][None].astype(F32), ln1_b[i][None].astype(F32)
        g2, b2 = ln2_g[i][None].astype(F32), ln2_b[i][None].astype(F32)
        g3, b3 = ln3_g[i][None].astype(F32), ln3_b[i][None].astype(F32)
        n_groups, pg, _ = pool_w[i].shape

        h1_head, meta1, wgu1, wd1 = _ffn_ln_head(
            h, meta, ffn1_w_gu[i], ffn1_w_down[i], g1, b1, tm=tm)
        jobs = (
            _CastJob(ffn2_w_gu[i], (32, ffn2_w_gu[i].shape[1] // 2), 0, panel=V7X_MXU_COLS),
            _CastJob(ffn2_w_down[i], (64, d), 0),
            _CastJob(w_out[i], (64, d), ffn2_w_down[i].shape[0] // 64),
            _CastJob(w_in[i], (32, w_in[i].shape[1]), 0, panel=V7X_MXU_COLS),
            _CastJob(pool_w[i].reshape(n_groups * pg, pg), (128, pg), w_in[i].shape[0] // 32),
        )
        h1, (wgu2, wd2, w_out_b, w_in_b, pool_w_b) = _ffn_ln(
            h, wgu1, wd1, g1, b1, tm=768, tail=h1_head, cast_jobs=jobs)
        pool_w_b = pool_w_b.reshape(n_groups, pg, pg)

        h2 = _mixer_ln(h1, meta1, w_in_b, conv_w[i], pool_w_b, pool_scale[i][None], w_out_b, g2, b2,
                       ts=1024, seq=seq)
        h, _ = _ffn_ln(h2, wgu2, wd2, g3, b3, tm=1024)
        assert DEPTH == 1
    return h.reshape(b, seq, d)
```

```python
import functools
from typing import NamedTuple

import jax
import jax.numpy as jnp
from jax import lax
from jax.experimental import pallas as pl
from jax.experimental.pallas import tpu as pltpu

N_META = 16
CONV_K = 3
POOL_WINDOWS = (2, 4, 8, 16)
LN_EPS = 1e-5
DEPTH = 1
ALPHA = (2.0 * DEPTH) ** 0.25

BF16 = jnp.bfloat16
F32 = jnp.float32

V7X_MXU_COLS = 256
V7X_VMEM_LIMIT_BYTES = 60 * 1024 * 1024
HALO = 16
LN_ROWS = 8
LN_UNROLL = 16
FFN_DOWN_COLS = 512
X_SLAB_COLS = 128
MIXER_SLAB_COLS = 256
PAIR_SLAB_COLS = 256
PAIR_STAGE_ROWS = 64
V7X_VMEM_LIMIT_MAX_BYTES = 63 * 1024 * 1024
OUT_COPIES_IN_FLIGHT = 4


def _layer_norm_rows(y, g, b):
    mu = jnp.mean(y, axis=-1, keepdims=True)
    yc = y - mu
    var = jnp.mean(yc * yc, axis=-1, keepdims=True)
    return yc * lax.rsqrt(var + LN_EPS) * g + b


def _ln_group_rows(rows):
    return min(LN_ROWS * LN_UNROLL, rows)


def _layer_norm_group(o_ref, r0, res_ref, branch_scale, g_ref, b_ref):
    group = _ln_group_rows(o_ref.shape[0])
    outs = []
    for u in range(0, group, LN_ROWS):
        sub = pl.ds(r0 + u, LN_ROWS)
        y = o_ref[sub, :]
        if branch_scale is not None:
            y = branch_scale * y
        if res_ref is not None:
            y = ALPHA * res_ref[sub, :] + y
        outs.append(_layer_norm_rows(y, g_ref[...], b_ref[...]))
    for k, u in enumerate(range(0, group, LN_ROWS)):
        o_ref[pl.ds(r0 + u, LN_ROWS), :] = outs[k]


def _residual_layer_norm(o_ref, res_ref, branch_scale, g_ref, b_ref):
    rows = o_ref.shape[0]
    group = _ln_group_rows(rows)

    def body(r, carry):
        _layer_norm_group(o_ref, pl.multiple_of(r * group, group), res_ref, branch_scale, g_ref, b_ref)
        return carry

    lax.fori_loop(0, rows // group, body, 0)


def _swiglu_chunk(xb, wgu):
    tf = wgu.shape[1] // 2
    gu = jnp.dot(xb, wgu, preferred_element_type=F32)
    gate, up = gu[:, :tf], gu[:, tf:]
    return (gate * jax.nn.sigmoid(gate) * up).astype(BF16)


class _CastJob(NamedTuple):
    src: jax.Array
    block: tuple
    start: int
    panel: int = 0

    @property
    def n_blocks(self):
        return (self.src.shape[0] // self.block[0]) * (self.src.shape[1] // self.block[1])

    @property
    def out_shape(self):
        rows, cols = self.src.shape
        col_blocks = cols // self.block[1]
        shape = (self.block[1] // self.panel, rows, col_blocks * self.panel) if self.panel else (rows, cols)
        return jax.ShapeDtypeStruct(shape, BF16)

    def _block_index(self, steps_per_row_tile, i, j):
        col_blocks = self.src.shape[1] // self.block[1]
        s = jnp.clip(i * steps_per_row_tile + j - self.start, 0, self.n_blocks - 1)
        return s // col_blocks, s % col_blocks

    def in_spec(self, steps_per_row_tile):
        return pl.BlockSpec(self.block, functools.partial(self._block_index, steps_per_row_tile))

    def out_spec(self, steps_per_row_tile):
        if not self.panel:
            return self.in_spec(steps_per_row_tile)
        br, bc = self.block

        def index_map(i, j):
            rb, cb = self._block_index(steps_per_row_tile, i, j)
            return (0, rb, cb)

        return pl.BlockSpec((bc // self.panel, br, self.panel), index_map)


def _cast_block(src_ref, dst_ref):
    if len(dst_ref.shape) == 2:
        dst_ref[...] = src_ref[...].astype(BF16)
    else:
        panel = dst_ref.shape[2]
        for p in range(dst_ref.shape[0]):
            dst_ref[p] = src_ref[:, p * panel:(p + 1) * panel].astype(BF16)


def _ffn_kernel(*refs, n_tiles, tm, cast_ranges):
    n_cast = len(cast_ranges)
    n_fixed = 6
    xres_ref, xnext_ref, wgu_ref, wd_ref, g_ref, b_ref = refs[:n_fixed]
    cast_in = refs[n_fixed:n_fixed + n_cast]
    n_in = len(refs) - (5 + n_cast)
    tail_hbm = refs[n_fixed + n_cast] if n_in > n_fixed + n_cast else None
    out_hbm = refs[n_in]
    cast_out = refs[n_in + 1:n_in + 1 + n_cast]
    acc_ref, xb_ref, sem, tail_sem = refs[n_in + 1 + n_cast:]
    i = pl.program_id(0)
    j = pl.program_id(1)
    nf = pl.num_programs(1)
    d = acc_ref.shape[1]
    n_slabs = d // X_SLAB_COLS
    slab = pl.ds(pl.multiple_of(jnp.minimum(j, n_slabs - 1) * X_SLAB_COLS, X_SLAB_COLS), X_SLAB_COLS)

    if tail_hbm is not None:
        tail_copy = pltpu.make_async_copy(
            tail_hbm, out_hbm.at[pl.ds(n_tiles * tm, tail_hbm.shape[0])], tail_sem)

        @pl.when(jnp.logical_and(i == 0, j == 0))
        def _():
            tail_copy.start()

        @pl.when(jnp.logical_and(i == n_tiles, j == nf - 1))
        def _():
            tail_copy.wait()

    @pl.when(jnp.logical_and(i < n_tiles, j < n_slabs))
    def _():
        xb_ref.at[i % 2][:, slab] = xnext_ref[...].astype(BF16)

    step = (i - 1) * nf + j
    for src_ref, dst_ref, (start, stop) in zip(cast_in, cast_out, cast_ranges):
        @pl.when(jnp.logical_and(step >= start, step < stop))
        def _(src_ref=src_ref, dst_ref=dst_ref):
            _cast_block(src_ref, dst_ref)

    @pl.when(i > 0)
    def _():
        @pl.when(j == 0)
        def _():
            acc_ref[...] = jnp.zeros_like(acc_ref)

        @pl.when(j < n_slabs)
        def _():
            acc_ref[:, slab] += (ALPHA / 0.5) * xres_ref[...]

        act = _swiglu_chunk(xb_ref[(i - 1) % 2], wgu_ref[...])
        for n0 in range(0, d, FFN_DOWN_COLS):
            cols = slice(n0, n0 + FFN_DOWN_COLS)
            acc_ref[:, cols] += jnp.dot(act, wd_ref[:, cols], preferred_element_type=F32)

        @pl.when(j == nf - 1)
        def _():
            _layer_norm_and_write_back(acc_ref, out_hbm, sem, (i - 1) * tm, 0.5, g_ref, b_ref)


def _layer_norm_and_write_back(acc_ref, out_hbm, sem, row0, branch_scale, g_ref, b_ref):
    rows = acc_ref.shape[0]
    group = _ln_group_rows(rows)
    n_groups = rows // group
    in_flight = min(OUT_COPIES_IN_FLIGHT, n_groups)

    def group_copy(k):
        r0 = pl.multiple_of(k * group, group)
        return pltpu.make_async_copy(
            acc_ref.at[pl.ds(r0, group)], out_hbm.at[pl.ds(row0 + r0, group)], sem.at[k % in_flight])

    def body(k, carry):
        @pl.when(k >= in_flight)
        def _():
            group_copy(k - in_flight).wait()

        _layer_norm_group(acc_ref, pl.multiple_of(k * group, group), None, branch_scale, g_ref, b_ref)
        group_copy(k).start()
        return carry

    lax.fori_loop(0, n_groups, body, 0)
    for k in range(n_groups - in_flight, n_groups):
        group_copy(k).wait()


def _ffn_ln(x, wgu, w_down, ln_g, ln_b, *, tm, tail=None, cast_jobs=()):
    m, d = x.shape
    f = w_down.shape[0]
    tf = wgu.shape[2] // 2
    n_tiles = (m - (0 if tail is None else tail.shape[0])) // tm
    assert n_tiles * tm + (0 if tail is None else tail.shape[0]) == m
    assert f % tf == 0 and tf == V7X_MXU_COLS and tm % _ln_group_rows(tm) == 0
    nf = f // tf
    n_slabs = d // X_SLAB_COLS
    assert n_slabs <= nf
    for job in cast_jobs:
        assert job.start + job.n_blocks <= n_tiles * nf

    def weight_chunk(i, j):
        return jnp.where(i == 0, 0, j)

    def slab_col(j):
        return jnp.minimum(j, n_slabs - 1)

    shifted_jobs = [job._replace(start=job.start + nf) for job in cast_jobs]
    operands = [x, x, wgu, w_down, ln_g, ln_b] + [job.src for job in cast_jobs]
    if tail is not None:
        operands.append(tail)
    outs = pl.pallas_call(
        functools.partial(
            _ffn_kernel, n_tiles=n_tiles, tm=tm,
            cast_ranges=tuple((job.start, job.start + job.n_blocks) for job in cast_jobs)),
        grid=(n_tiles + 1, nf),
        in_specs=[
            pl.BlockSpec((tm, X_SLAB_COLS), lambda i, j: (jnp.maximum(i - 1, 0), slab_col(j))),
            pl.BlockSpec((tm, X_SLAB_COLS), lambda i, j: (jnp.minimum(i, n_tiles - 1), slab_col(j))),
            pl.BlockSpec((None, d, 2 * tf), lambda i, j: (weight_chunk(i, j), 0, 0)),
            pl.BlockSpec((tf, d), lambda i, j: (weight_chunk(i, j), 0)),
            pl.BlockSpec((1, d), lambda i, j: (0, 0)),
            pl.BlockSpec((1, d), lambda i, j: (0, 0)),
        ] + [job.in_spec(nf) for job in shifted_jobs]
        + ([pl.BlockSpec(memory_space=pl.ANY)] if tail is not None else []),
        out_specs=[pl.BlockSpec(memory_space=pl.ANY)] + [job.out_spec(nf) for job in shifted_jobs],
        out_shape=[jax.ShapeDtypeStruct((m, d), F32)] + [job.out_shape for job in cast_jobs],
        scratch_shapes=[
            pltpu.VMEM((tm, d), F32),
            pltpu.VMEM((2, tm, d), BF16),
            pltpu.SemaphoreType.DMA((OUT_COPIES_IN_FLIGHT,)),
            pltpu.SemaphoreType.DMA(()),
        ],
        compiler_params=pltpu.CompilerParams(
            dimension_semantics=("arbitrary", "arbitrary"),
            vmem_limit_bytes=V7X_VMEM_LIMIT_BYTES,
        ),
        name="ffn_ln",
    )(*operands)
    return outs[0], tuple(outs[1:])


def _ffn_pair_kernel(x_hbm, xres_ref, wgu_ref, wd_ref, g_ref, b_ref, out_hbm,
                     acc_ref, xb_ref, stage_ref, sem, stage_sem, *, tm, n_chunks):
    i = pl.program_id(0)
    j = pl.program_id(1)
    n_steps = pl.num_programs(1)
    d = acc_ref.shape[1]
    tf = wd_ref.shape[0] // 2
    n_slabs = d // PAIR_SLAB_COLS
    slab = pl.ds(
        pl.multiple_of(jnp.minimum(j, n_slabs - 1) * PAIR_SLAB_COLS, PAIR_SLAB_COLS), PAIR_SLAB_COLS)
    rows = stage_ref.shape[1]
    n_groups = tm // rows

    def stage_copy(k):
        return pltpu.make_async_copy(
            x_hbm.at[pl.ds(i * tm + k * rows, rows)], stage_ref.at[k % 2], stage_sem.at[k % 2])

    @pl.when(j == 0)
    def _():
        stage_copy(0).start()
        acc_ref[...] = jnp.zeros_like(acc_ref)
        for k in range(n_groups):
            if k + 1 < n_groups:
                stage_copy(k + 1).start()
            stage_copy(k).wait()
            xb_ref[k * rows:(k + 1) * rows, :] = stage_ref[k % 2].astype(BF16)

    @pl.when(j < n_slabs)
    def _():
        acc_ref[:, slab] += (ALPHA / 0.5) * xres_ref[...]

    def chunk_group(n):
        xb = xb_ref[...]
        acts = [_swiglu_chunk(xb, wgu_ref[k]) for k in range(n)]
        act = acts[0] if n == 1 else jnp.concatenate(acts, axis=1)
        for n0 in range(0, d, FFN_DOWN_COLS):
            cols = slice(n0, n0 + FFN_DOWN_COLS)
            acc_ref[:, cols] += jnp.dot(act, wd_ref[0:n * tf, cols], preferred_element_type=F32)

    last_n = 2 - n_chunks % 2

    @pl.when(j < n_steps - 1)
    def _():
        chunk_group(2)

    @pl.when(j == n_steps - 1)
    def _():
        chunk_group(last_n)
        _layer_norm_and_write_back(acc_ref, out_hbm, sem, i * tm, 0.5, g_ref, b_ref)


def _ffn_ln_pairs(x, wgu, w_down, ln_g, ln_b, *, tm):
    m, d = x.shape
    f = w_down.shape[0]
    tf = wgu.shape[2] // 2
    n_chunks = f // tf
    assert m % tm == 0 and f % tf == 0 and tf == V7X_MXU_COLS and tm % _ln_group_rows(tm) == 0
    n_steps = -(-n_chunks // 2)
    n_slabs = d // PAIR_SLAB_COLS
    assert n_slabs <= n_steps and tm % PAIR_STAGE_ROWS == 0
    return pl.pallas_call(
        functools.partial(_ffn_pair_kernel, tm=tm, n_chunks=n_chunks),
        grid=(m // tm, n_steps),
        in_specs=[
            pl.BlockSpec(memory_space=pl.ANY),
            pl.BlockSpec((tm, PAIR_SLAB_COLS), lambda i, j: (i, jnp.minimum(j, n_slabs - 1))),
            pl.BlockSpec((2, d, 2 * tf), lambda i, j: (j, 0, 0)),
            pl.BlockSpec((2 * tf, d), lambda i, j: (j, 0)),
            pl.BlockSpec((1, d), lambda i, j: (0, 0)),
            pl.BlockSpec((1, d), lambda i, j: (0, 0)),
        ],
        out_specs=pl.BlockSpec(memory_space=pl.ANY),
        out_shape=jax.ShapeDtypeStruct((m, d), F32),
        scratch_shapes=[
            pltpu.VMEM((tm, d), F32),
            pltpu.VMEM((tm, d), BF16),
            pltpu.VMEM((2, PAIR_STAGE_ROWS, d), F32),
            pltpu.SemaphoreType.DMA((OUT_COPIES_IN_FLIGHT,)),
            pltpu.SemaphoreType.DMA((2,)),
        ],
        compiler_params=pltpu.CompilerParams(
            dimension_semantics=("arbitrary", "arbitrary"),
            vmem_limit_bytes=V7X_VMEM_LIMIT_MAX_BYTES,
        ),
        name="ffn_ln_pairs",
    )(x, x, wgu, w_down, ln_g, ln_b)


def _ffn_head_kernel(x_ref, meta_ref, wg_ref, wu_ref, wd_ref, g_ref, b_ref,
                     o_ref, om_ref, wgub_ref, wdb_ref, xb_ref):
    j = pl.program_id(0)
    n_meta = meta_ref.shape[0]

    @pl.when(j == 0)
    def _():
        xb_ref[0:n_meta, :] = meta_ref[...].astype(BF16)
        xb_ref[n_meta:, :] = x_ref[...].astype(BF16)
        o_ref[...] = jnp.zeros_like(o_ref)
        om_ref[...] = jnp.zeros_like(om_ref)

    tf = wg_ref.shape[1]
    wgub_ref[:, 0:tf] = wg_ref[...].astype(BF16)
    wgub_ref[:, tf:] = wu_ref[...].astype(BF16)
    wdb_ref[...] = wd_ref[...].astype(BF16)
    act = _swiglu_chunk(xb_ref[...], wgub_ref[...])
    d = o_ref.shape[1]
    for n0 in range(0, d, FFN_DOWN_COLS):
        cols = slice(n0, n0 + FFN_DOWN_COLS)
        part = jnp.dot(act, wdb_ref[:, cols], preferred_element_type=F32)
        om_ref[:, cols] += part[0:n_meta, :]
        o_ref[:, cols] += part[n_meta:, :]

    @pl.when(j == pl.num_programs(0) - 1)
    def _():
        _residual_layer_norm(om_ref, meta_ref, 0.5, g_ref, b_ref)
        _residual_layer_norm(o_ref, x_ref, 0.5, g_ref, b_ref)


def _ffn_ln_head(x, meta, w_gu, w_down, ln_g, ln_b, *, tm):
    m, d = x.shape
    n_meta = meta.shape[0]
    f = w_down.shape[0]
    tf = V7X_MXU_COLS
    nf = f // tf
    assert m % tm == 0
    last = m // tm - 1
    resident = pl.Buffered(1)
    return pl.pallas_call(
        _ffn_head_kernel,
        grid=(nf,),
        in_specs=[
            pl.BlockSpec((tm, d), lambda j: (last, 0), pipeline_mode=resident),
            pl.BlockSpec((n_meta, d), lambda j: (0, 0), pipeline_mode=resident),
            pl.BlockSpec((d, tf), lambda j: (0, j)),
            pl.BlockSpec((d, tf), lambda j: (0, nf + j)),
            pl.BlockSpec((tf, d), lambda j: (j, 0)),
            pl.BlockSpec((1, d), lambda j: (0, 0)),
            pl.BlockSpec((1, d), lambda j: (0, 0)),
        ],
        out_specs=[
            pl.BlockSpec((tm, d), lambda j: (0, 0), pipeline_mode=resident),
            pl.BlockSpec((n_meta, d), lambda j: (0, 0), pipeline_mode=resident),
            pl.BlockSpec((None, d, 2 * tf), lambda j: (j, 0, 0)),
            pl.BlockSpec((tf, d), lambda j: (j, 0)),
        ],
        out_shape=[
            jax.ShapeDtypeStruct((tm, d), F32),
            jax.ShapeDtypeStruct((n_meta, d), F32),
            jax.ShapeDtypeStruct((nf, d, 2 * tf), BF16),
            jax.ShapeDtypeStruct((f, d), BF16),
        ],
        scratch_shapes=[pltpu.VMEM((n_meta + tm, d), BF16)],
        compiler_params=pltpu.CompilerParams(
            dimension_semantics=("arbitrary",),
            vmem_limit_bytes=V7X_VMEM_LIMIT_BYTES,
        ),
        name="ffn_ln_head",
    )(x, meta, w_gu, w_gu, w_down, ln_g, ln_b)


def _mixer_kernel(hres_ref, hnext_ref, halo_ref, meta_ref, wa_ref, wb_ref, wc_ref, wo_ref, pw_ref,
                  cw_ref, ps_ref, g_ref, b_ref, out_hbm, acc_ref, xb_ref, yp_ref, sem,
                  *, n_tiles, ts, tiles_per_seq, n_conv):
    i = pl.program_id(0)
    c = pl.program_id(1)
    d = acc_ref.shape[1]
    tc = wa_ref.shape[1]
    n_slabs = d // MIXER_SLAB_COLS
    slab = pl.ds(
        pl.multiple_of(jnp.minimum(c, n_slabs - 1) * MIXER_SLAB_COLS, MIXER_SLAB_COLS), MIXER_SLAB_COLS)

    def round_next_slab():
        first = (i % tiles_per_seq) == 0
        nxt = xb_ref.at[i % 2]
        nxt[0:HALO, slab] = jnp.where(first, meta_ref[...], halo_ref[...]).astype(BF16)
        nxt[HALO:, slab] = hnext_ref[...].astype(BF16)

    def accumulate(y):
        for n0 in range(0, d, FFN_DOWN_COLS):
            cols = slice(n0, n0 + FFN_DOWN_COLS)
            acc_ref[:, cols] += jnp.dot(y, wo_ref[:, cols], preferred_element_type=F32)

    @pl.when(jnp.logical_and(i < n_tiles, c < n_slabs))
    def _():
        round_next_slab()

    @pl.when(i > 0)
    def _():
        @pl.when(c == 0)
        def _():
            acc_ref[...] = jnp.zeros_like(acc_ref)

        @pl.when(c < n_slabs)
        def _():
            acc_ref[:, slab] += ALPHA * hres_ref[...]

        pool_step = c - n_conv
        group = pool_step // 2

        @pl.when(c < n_conv)
        def _():
            half = HALO + (ts // 2)
            hb_ref = xb_ref.at[(i - 1) % 2]
            tops = [jnp.dot(hb_ref[0:half, :], w[...], preferred_element_type=F32)
                    for w in (wa_ref, wb_ref, wc_ref)]
            bots = [jnp.dot(hb_ref[half:, :], w[...], preferred_element_type=F32)
                    for w in (wa_ref, wb_ref, wc_ref)]
            gate_b, gate_c, x_in = [jnp.concatenate(tb, axis=0) for tb in zip(tops, bots)]
            v = gate_c * x_in
            conv = pltpu.roll(v, 2, 0) * cw_ref[0:1, :]
            conv = conv + pltpu.roll(v, 1, 0) * cw_ref[1:2, :]
            conv = conv + v * cw_ref[2:3, :]
            accumulate((gate_b * conv)[HALO:, :].astype(BF16))

        @pl.when(jnp.logical_and(pool_step >= 0, pool_step % 2 == 0))
        def _():
            hb = xb_ref[(i - 1) % 2]
            z = jnp.concatenate([jnp.dot(hb, wa_ref[...], preferred_element_type=F32),
                                 jnp.dot(hb, wb_ref[...], preferred_element_type=F32)], axis=1)
            s = z
            for level in range(len(POOL_WINDOWS)):
                doubled = s + pltpu.roll(s, 1 << level, 0)
                s = doubled if level == 0 else jnp.where(level <= group, doubled, s)
            inv_w = jnp.where(group == 0, 1.0 / POOL_WINDOWS[0],
                              jnp.where(group == 1, 1.0 / POOL_WINDOWS[1],
                                        jnp.where(group == 2, 1.0 / POOL_WINDOWS[2],
                                                  1.0 / POOL_WINDOWS[3])))
            diff = (s * inv_w.astype(F32) - z)[HALO:, :].astype(BF16)
            y = (jnp.dot(diff, pw_ref[...], preferred_element_type=F32) * ps_ref[...]).astype(BF16)
            yp_ref[...] = y
            accumulate(y[:, 0:tc])

        @pl.when(jnp.logical_and(pool_step >= 0, pool_step % 2 == 1))
        def _():
            accumulate(yp_ref[:, tc:])

        @pl.when(c == pl.num_programs(1) - 1)
        def _():
            _layer_norm_and_write_back(acc_ref, out_hbm, sem, (i - 1) * ts, None, g_ref, b_ref)


def _mixer_ln(h, meta_h, w_in, conv_w, pool_w, pool_scale, w_out, ln_g, ln_b, *, ts, seq):
    m, d = h.shape
    cc = conv_w.shape[1]
    n_groups, pg, _ = pool_w.shape
    n_panels, _, tc = w_in.shape
    n_conv = cc // tc
    assert pg == 2 * tc and n_panels == 3 * n_conv + 2 * n_groups and w_out.shape[0] == cc + n_groups * pg
    n_steps = n_conv + 2 * n_groups
    n_tiles = m // ts
    tiles_per_seq = seq // ts
    n_slabs = d // MIXER_SLAB_COLS
    assert m % ts == 0 and seq % ts == 0 and n_slabs <= n_steps and ts % _ln_group_rows(ts) == 0

    def step(i, c):
        return jnp.where(i == 0, 0, c)

    def group(i, c):
        return jnp.maximum(step(i, c) - n_conv, 0) // 2

    def slab_col(c):
        return jnp.minimum(c, n_slabs - 1)

    def next_tile(i):
        return jnp.minimum(i, n_tiles - 1)

    def pool_or(i, c, conv_panel, pool_panel):
        return jnp.where(step(i, c) < n_conv, conv_panel, pool_panel)

    return pl.pallas_call(
        functools.partial(
            _mixer_kernel, n_tiles=n_tiles, ts=ts, tiles_per_seq=tiles_per_seq, n_conv=n_conv),
        grid=(n_tiles + 1, n_steps),
        in_specs=[
            pl.BlockSpec((ts, MIXER_SLAB_COLS), lambda i, c: (jnp.maximum(i - 1, 0), slab_col(c))),
            pl.BlockSpec((ts, MIXER_SLAB_COLS), lambda i, c: (next_tile(i), slab_col(c))),
            pl.BlockSpec((HALO, MIXER_SLAB_COLS),
                         lambda i, c: (jnp.maximum(next_tile(i) * (ts // HALO) - 1, 0), slab_col(c))),
            pl.BlockSpec((HALO, MIXER_SLAB_COLS), lambda i, c: (0, slab_col(c))),
            pl.BlockSpec((None, d, tc), lambda i, c: (
                pool_or(i, c, step(i, c), 3 * n_conv + 2 * group(i, c)), 0, 0)),
            pl.BlockSpec((None, d, tc), lambda i, c: (
                pool_or(i, c, n_conv + step(i, c), 3 * n_conv + 2 * group(i, c) + 1), 0, 0)),
            pl.BlockSpec((None, d, tc), lambda i, c: (
                2 * n_conv + jnp.minimum(step(i, c), n_conv - 1), 0, 0)),
            pl.BlockSpec((tc, d), lambda i, c: (step(i, c), 0)),
            pl.BlockSpec((None, pg, pg), lambda i, c: (group(i, c), 0, 0)),
            pl.BlockSpec((CONV_K, tc), lambda i, c: (0, jnp.minimum(step(i, c), n_conv - 1))),
            pl.BlockSpec((1, pg), lambda i, c: (0, group(i, c))),
            pl.BlockSpec((1, d), lambda i, c: (0, 0)),
            pl.BlockSpec((1, d), lambda i, c: (0, 0)),
        ],
        out_specs=pl.BlockSpec(memory_space=pl.ANY),
        out_shape=jax.ShapeDtypeStruct((m, d), F32),
        scratch_shapes=[
            pltpu.VMEM((ts, d), F32),
            pltpu.VMEM((2, HALO + ts, d), BF16),
            pltpu.VMEM((ts, pg), BF16),
            pltpu.SemaphoreType.DMA((OUT_COPIES_IN_FLIGHT,)),
        ],
        compiler_params=pltpu.CompilerParams(
            dimension_semantics=("arbitrary", "arbitrary"),
            vmem_limit_bytes=V7X_VMEM_LIMIT_BYTES,
        ),
        name="mixer_ln",
    )(h, h, h, meta_h, w_in, w_in, w_in, w_out, pool_w, conv_w, pool_scale, ln_g, ln_b)


def kernel(x, meta_tokens, ffn1_w_gu, ffn1_w_down, ln1_g, ln1_b, w_in, conv_w, pool_w, pool_scale,
           w_out, ln2_g, ln2_b, ffn2_w_gu, ffn2_w_down, ln3_g, ln3_b):
    b, seq, d = x.shape
    assert meta_tokens.shape == (N_META, d) and N_META == HALO
    assert ffn1_w_gu.shape[0] == DEPTH
    tm = 512

    h = x.reshape(b * seq, d)
    meta = meta_tokens.astype(x.dtype)
    for i in range(DEPTH):
        g1, b1 = ln1_g[i][None].astype(F32), ln1_b[i][None].astype(F32)
        g2, b2 = ln2_g[i][None].astype(F32), ln2_b[i][None].astype(F32)
        g3, b3 = ln3_g[i][None].astype(F32), ln3_b[i][None].astype(F32)
        n_groups, pg, _ = pool_w[i].shape

        h1_head, meta1, wgu1, wd1 = _ffn_ln_head(
            h, meta, ffn1_w_gu[i], ffn1_w_down[i], g1, b1, tm=tm)
        jobs = (
            _CastJob(ffn2_w_gu[i], (32, ffn2_w_gu[i].shape[1] // 2), 0, panel=V7X_MXU_COLS),
            _CastJob(ffn2_w_down[i], (64, d), 0),
            _CastJob(w_out[i], (64, d), ffn2_w_down[i].shape[0] // 64),
            _CastJob(w_in[i], (32, w_in[i].shape[1]), 0, panel=V7X_MXU_COLS),
            _CastJob(pool_w[i].reshape(n_groups * pg, pg), (128, pg), w_in[i].shape[0] // 32),
        )
        h1, (wgu2, wd2, w_out_b, w_in_b, pool_w_b) = _ffn_ln(
            h, wgu1, wd1, g1, b1, tm=768, tail=h1_head, cast_jobs=jobs)
        pool_w_b = pool_w_b.reshape(n_groups, pg, pg)

        h2 = _mixer_ln(h1, meta1, w_in_b, conv_w[i], pool_w_b, pool_scale[i][None], w_out_b, g2, b2,
                       ts=1024, seq=seq)
        h = _ffn_ln_pairs(h2, wgu2, wd2, g3, b3, tm=1024)
        assert DEPTH == 1
    return h.reshape(b, seq, d)
```

```python
import functools
from typing import NamedTuple

import jax
import jax.numpy as jnp
from jax import lax
from jax.experimental import pallas as pl
from jax.experimental.pallas import tpu as pltpu

N_META = 16
CONV_K = 3
POOL_WINDOWS = (2, 4, 8, 16)
LN_EPS = 1e-5
DEPTH = 1
ALPHA = (2.0 * DEPTH) ** 0.25

BF16 = jnp.bfloat16
F32 = jnp.float32

V7X_MXU_COLS = 256
V7X_VMEM_LIMIT_BYTES = 60 * 1024 * 1024
HALO = 16
LN_ROWS = 8
LN_UNROLL = 16
FFN_DOWN_COLS = 512
X_SLAB_COLS = 128
MIXER_SLAB_COLS = 256
OUT_COPIES_IN_FLIGHT = 4

HEAD_ROWS = 512
FFN1_TILE_ROWS = 768
FFN2_TILE_ROWS = 1024
MIXER_TILE_ROWS = 1024


def _layer_norm_rows(y, g, b):
    mu = jnp.mean(y, axis=-1, keepdims=True)
    yc = y - mu
    var = jnp.mean(yc * yc, axis=-1, keepdims=True)
    return yc * lax.rsqrt(var + LN_EPS) * g + b


def _ln_group_rows(rows):
    return min(LN_ROWS * LN_UNROLL, rows)


def _layer_norm_group(o_ref, r0, res_ref, branch_scale, g_ref, b_ref):
    group = _ln_group_rows(o_ref.shape[0])
    outs = []
    for u in range(0, group, LN_ROWS):
        sub = pl.ds(r0 + u, LN_ROWS)
        y = o_ref[sub, :]
        if branch_scale is not None:
            y = branch_scale * y
        if res_ref is not None:
            y = ALPHA * res_ref[sub, :] + y
        outs.append(_layer_norm_rows(y, g_ref[...], b_ref[...]))
    for k, u in enumerate(range(0, group, LN_ROWS)):
        o_ref[pl.ds(r0 + u, LN_ROWS), :] = outs[k]


def _residual_layer_norm(o_ref, res_ref, branch_scale, g_ref, b_ref):
    rows = o_ref.shape[0]
    group = _ln_group_rows(rows)

    def body(r, carry):
        _layer_norm_group(o_ref, pl.multiple_of(r * group, group), res_ref, branch_scale, g_ref, b_ref)
        return carry

    lax.fori_loop(0, rows // group, body, 0)


def _swiglu_chunk(xb, wgu):
    tf = wgu.shape[1] // 2
    gu = jnp.dot(xb, wgu, preferred_element_type=F32)
    gate, up = gu[:, :tf], gu[:, tf:]
    return (gate * jax.nn.sigmoid(gate) * up).astype(BF16)


class _CastJob(NamedTuple):
    src: jax.Array
    block: tuple
    start: int
    panel: int = 0

    @property
    def n_blocks(self):
        return (self.src.shape[0] // self.block[0]) * (self.src.shape[1] // self.block[1])

    @property
    def out_shape(self):
        rows, cols = self.src.shape
        col_blocks = cols // self.block[1]
        shape = (self.block[1] // self.panel, rows, col_blocks * self.panel) if self.panel else (rows, cols)
        return jax.ShapeDtypeStruct(shape, BF16)

    def _block_index(self, steps_per_row_tile, i, j):
        col_blocks = self.src.shape[1] // self.block[1]
        s = jnp.clip(i * steps_per_row_tile + j - self.start, 0, self.n_blocks - 1)
        return s // col_blocks, s % col_blocks

    def in_spec(self, steps_per_row_tile):
        return pl.BlockSpec(self.block, functools.partial(self._block_index, steps_per_row_tile))

    def out_spec(self, steps_per_row_tile):
        if not self.panel:
            return self.in_spec(steps_per_row_tile)
        br, bc = self.block

        def index_map(i, j):
            rb, cb = self._block_index(steps_per_row_tile, i, j)
            return (0, rb, cb)

        return pl.BlockSpec((bc // self.panel, br, self.panel), index_map)


def _cast_block(src_ref, dst_ref):
    if len(dst_ref.shape) == 2:
        dst_ref[...] = src_ref[...].astype(BF16)
    else:
        panel = dst_ref.shape[2]
        for p in range(dst_ref.shape[0]):
            dst_ref[p] = src_ref[:, p * panel:(p + 1) * panel].astype(BF16)


def _ffn_kernel(*refs, n_tiles, tm, cast_ranges):
    n_cast = len(cast_ranges)
    n_fixed = 6
    xres_ref, xnext_ref, wgu_ref, wd_ref, g_ref, b_ref = refs[:n_fixed]
    cast_in = refs[n_fixed:n_fixed + n_cast]
    n_in = len(refs) - (5 + n_cast)
    tail_hbm = refs[n_fixed + n_cast] if n_in > n_fixed + n_cast else None
    out_hbm = refs[n_in]
    cast_out = refs[n_in + 1:n_in + 1 + n_cast]
    acc_ref, xb_ref, sem, tail_sem = refs[n_in + 1 + n_cast:]
    i = pl.program_id(0)
    j = pl.program_id(1)
    nf = pl.num_programs(1)
    d = acc_ref.shape[1]
    n_slabs = d // X_SLAB_COLS
    slab = pl.ds(pl.multiple_of(jnp.minimum(j, n_slabs - 1) * X_SLAB_COLS, X_SLAB_COLS), X_SLAB_COLS)

    if tail_hbm is not None:
        tail_copy = pltpu.make_async_copy(
            tail_hbm, out_hbm.at[pl.ds(n_tiles * tm, tail_hbm.shape[0])], tail_sem)

        @pl.when(jnp.logical_and(i == 0, j == 0))
        def _():
            tail_copy.start()

        @pl.when(jnp.logical_and(i == n_tiles, j == nf - 1))
        def _():
            tail_copy.wait()

    @pl.when(jnp.logical_and(i < n_tiles, j < n_slabs))
    def _():
        xb_ref.at[i % 2][:, slab] = xnext_ref[...].astype(BF16)

    step = (i - 1) * nf + j
    for src_ref, dst_ref, (start, stop) in zip(cast_in, cast_out, cast_ranges):
        @pl.when(jnp.logical_and(step >= start, step < stop))
        def _(src_ref=src_ref, dst_ref=dst_ref):
            _cast_block(src_ref, dst_ref)

    @pl.when(i > 0)
    def _():
        @pl.when(j == 0)
        def _():
            acc_ref[...] = jnp.zeros_like(acc_ref)

        @pl.when(j < n_slabs)
        def _():
            acc_ref[:, slab] += (ALPHA / 0.5) * xres_ref[...]

        act = _swiglu_chunk(xb_ref[(i - 1) % 2], wgu_ref[...])
        for n0 in range(0, d, FFN_DOWN_COLS):
            cols = slice(n0, n0 + FFN_DOWN_COLS)
            acc_ref[:, cols] += jnp.dot(act, wd_ref[:, cols], preferred_element_type=F32)

        @pl.when(j == nf - 1)
        def _():
            _layer_norm_and_write_back(acc_ref, out_hbm, sem, (i - 1) * tm, 0.5, g_ref, b_ref)


def _layer_norm_and_write_back(acc_ref, out_hbm, sem, row0, branch_scale, g_ref, b_ref):
    rows = acc_ref.shape[0]
    group = _ln_group_rows(rows)
    n_groups = rows // group
    in_flight = min(OUT_COPIES_IN_FLIGHT, n_groups)

    def group_copy(k):
        r0 = pl.multiple_of(k * group, group)
        return pltpu.make_async_copy(
            acc_ref.at[pl.ds(r0, group)], out_hbm.at[pl.ds(row0 + r0, group)], sem.at[k % in_flight])

    def body(k, carry):
        @pl.when(k >= in_flight)
        def _():
            group_copy(k - in_flight).wait()

        _layer_norm_group(acc_ref, pl.multiple_of(k * group, group), None, branch_scale, g_ref, b_ref)
        group_copy(k).start()
        return carry

    lax.fori_loop(0, n_groups, body, 0)
    for k in range(n_groups - in_flight, n_groups):
        group_copy(k).wait()


def _ffn_ln(x, wgu, w_down, ln_g, ln_b, *, tm, tail=None, cast_jobs=()):
    m, d = x.shape
    f = w_down.shape[0]
    tf = wgu.shape[2] // 2
    n_tiles = (m - (0 if tail is None else tail.shape[0])) // tm
    assert n_tiles * tm + (0 if tail is None else tail.shape[0]) == m
    assert f % tf == 0 and tf == V7X_MXU_COLS and tm % _ln_group_rows(tm) == 0
    nf = f // tf
    n_slabs = d // X_SLAB_COLS
    assert n_slabs <= nf
    for job in cast_jobs:
        assert job.start + job.n_blocks <= n_tiles * nf

    def weight_chunk(i, j):
        return jnp.where(i == 0, 0, j)

    def slab_col(j):
        return jnp.minimum(j, n_slabs - 1)

    shifted_jobs = [job._replace(start=job.start + nf) for job in cast_jobs]
    operands = [x, x, wgu, w_down, ln_g, ln_b] + [job.src for job in cast_jobs]
    if tail is not None:
        operands.append(tail)
    outs = pl.pallas_call(
        functools.partial(
            _ffn_kernel, n_tiles=n_tiles, tm=tm,
            cast_ranges=tuple((job.start, job.start + job.n_blocks) for job in cast_jobs)),
        grid=(n_tiles + 1, nf),
        in_specs=[
            pl.BlockSpec((tm, X_SLAB_COLS), lambda i, j: (jnp.maximum(i - 1, 0), slab_col(j))),
            pl.BlockSpec((tm, X_SLAB_COLS), lambda i, j: (jnp.minimum(i, n_tiles - 1), slab_col(j))),
            pl.BlockSpec((None, d, 2 * tf), lambda i, j: (weight_chunk(i, j), 0, 0)),
            pl.BlockSpec((tf, d), lambda i, j: (weight_chunk(i, j), 0)),
            pl.BlockSpec((1, d), lambda i, j: (0, 0)),
            pl.BlockSpec((1, d), lambda i, j: (0, 0)),
        ] + [job.in_spec(nf) for job in shifted_jobs]
        + ([pl.BlockSpec(memory_space=pl.ANY)] if tail is not None else []),
        out_specs=[pl.BlockSpec(memory_space=pl.ANY)] + [job.out_spec(nf) for job in shifted_jobs],
        out_shape=[jax.ShapeDtypeStruct((m, d), F32)] + [job.out_shape for job in cast_jobs],
        scratch_shapes=[
            pltpu.VMEM((tm, d), F32),
            pltpu.VMEM((2, tm, d), BF16),
            pltpu.SemaphoreType.DMA((OUT_COPIES_IN_FLIGHT,)),
            pltpu.SemaphoreType.DMA(()),
        ],
        compiler_params=pltpu.CompilerParams(
            dimension_semantics=("arbitrary", "arbitrary"),
            vmem_limit_bytes=V7X_VMEM_LIMIT_BYTES,
        ),
        name="ffn_ln",
    )(*operands)
    return outs[0], tuple(outs[1:])


def _ffn_head_kernel(x_ref, meta_ref, wg_ref, wu_ref, wd_ref, g_ref, b_ref,
                     o_ref, om_ref, wgub_ref, wdb_ref, xb_ref):
    j = pl.program_id(0)
    n_meta = meta_ref.shape[0]

    @pl.when(j == 0)
    def _():
        xb_ref[0:n_meta, :] = meta_ref[...].astype(BF16)
        xb_ref[n_meta:, :] = x_ref[...].astype(BF16)
        o_ref[...] = jnp.zeros_like(o_ref)
        om_ref[...] = jnp.zeros_like(om_ref)

    tf = wg_ref.shape[1]
    wgub_ref[:, 0:tf] = wg_ref[...].astype(BF16)
    wgub_ref[:, tf:] = wu_ref[...].astype(BF16)
    wdb_ref[...] = wd_ref[...].astype(BF16)
    act = _swiglu_chunk(xb_ref[...], wgub_ref[...])
    d = o_ref.shape[1]
    for n0 in range(0, d, FFN_DOWN_COLS):
        cols = slice(n0, n0 + FFN_DOWN_COLS)
        part = jnp.dot(act, wdb_ref[:, cols], preferred_element_type=F32)
        om_ref[:, cols] += part[0:n_meta, :]
        o_ref[:, cols] += part[n_meta:, :]

    @pl.when(j == pl.num_programs(0) - 1)
    def _():
        _residual_layer_norm(om_ref, meta_ref, 0.5, g_ref, b_ref)
        _residual_layer_norm(o_ref, x_ref, 0.5, g_ref, b_ref)


def _ffn_ln_head(x, meta, w_gu, w_down, ln_g, ln_b, *, tm):
    m, d = x.shape
    n_meta = meta.shape[0]
    f = w_down.shape[0]
    tf = V7X_MXU_COLS
    nf = f // tf
    assert m % tm == 0
    last = m // tm - 1
    resident = pl.Buffered(1)
    return pl.pallas_call(
        _ffn_head_kernel,
        grid=(nf,),
        in_specs=[
            pl.BlockSpec((tm, d), lambda j: (last, 0), pipeline_mode=resident),
            pl.BlockSpec((n_meta, d), lambda j: (0, 0), pipeline_mode=resident),
            pl.BlockSpec((d, tf), lambda j: (0, j)),
            pl.BlockSpec((d, tf), lambda j: (0, nf + j)),
            pl.BlockSpec((tf, d), lambda j: (j, 0)),
            pl.BlockSpec((1, d), lambda j: (0, 0)),
            pl.BlockSpec((1, d), lambda j: (0, 0)),
        ],
        out_specs=[
            pl.BlockSpec((tm, d), lambda j: (0, 0), pipeline_mode=resident),
            pl.BlockSpec((n_meta, d), lambda j: (0, 0), pipeline_mode=resident),
            pl.BlockSpec((None, d, 2 * tf), lambda j: (j, 0, 0)),
            pl.BlockSpec((tf, d), lambda j: (j, 0)),
        ],
        out_shape=[
            jax.ShapeDtypeStruct((tm, d), F32),
            jax.ShapeDtypeStruct((n_meta, d), F32),
            jax.ShapeDtypeStruct((nf, d, 2 * tf), BF16),
            jax.ShapeDtypeStruct((f, d), BF16),
        ],
        scratch_shapes=[pltpu.VMEM((n_meta + tm, d), BF16)],
        compiler_params=pltpu.CompilerParams(
            dimension_semantics=("arbitrary",),
            vmem_limit_bytes=V7X_VMEM_LIMIT_BYTES,
        ),
        name="ffn_ln_head",
    )(x, meta, w_gu, w_gu, w_down, ln_g, ln_b)


def _mixer_kernel(hres_ref, hnext_ref, halo_ref, meta_ref, wa_ref, wb_ref, wc_ref, wo_ref, pw_ref,
                  cw_ref, ps_ref, g_ref, b_ref, out_hbm, acc_ref, xb_ref, yp_ref, sem,
                  *, n_tiles, ts, tiles_per_seq, n_conv):
    i = pl.program_id(0)
    c = pl.program_id(1)
    d = acc_ref.shape[1]
    tc = wa_ref.shape[1]
    n_slabs = d // MIXER_SLAB_COLS
    slab = pl.ds(
        pl.multiple_of(jnp.minimum(c, n_slabs - 1) * MIXER_SLAB_COLS, MIXER_SLAB_COLS), MIXER_SLAB_COLS)

    def round_next_slab():
        first = (i % tiles_per_seq) == 0
        nxt = xb_ref.at[i % 2]
        nxt[0:HALO, slab] = jnp.where(first, meta_ref[...], halo_ref[...]).astype(BF16)
        nxt[HALO:, slab] = hnext_ref[...].astype(BF16)

    def accumulate(y):
        for n0 in range(0, d, FFN_DOWN_COLS):
            cols = slice(n0, n0 + FFN_DOWN_COLS)
            acc_ref[:, cols] += jnp.dot(y, wo_ref[:, cols], preferred_element_type=F32)

    @pl.when(jnp.logical_and(i < n_tiles, c < n_slabs))
    def _():
        round_next_slab()

    @pl.when(i > 0)
    def _():
        @pl.when(c == 0)
        def _():
            acc_ref[...] = jnp.zeros_like(acc_ref)

        @pl.when(c < n_slabs)
        def _():
            acc_ref[:, slab] += ALPHA * hres_ref[...]

        pool_step = c - n_conv
        group = pool_step // 2

        @pl.when(c < n_conv)
        def _():
            half = HALO + (ts // 2)
            hb_ref = xb_ref.at[(i - 1) % 2]
            tops = [jnp.dot(hb_ref[0:half, :], w[...], preferred_element_type=F32)
                    for w in (wa_ref, wb_ref, wc_ref)]
            bots = [jnp.dot(hb_ref[half:, :], w[...], preferred_element_type=F32)
                    for w in (wa_ref, wb_ref, wc_ref)]
            gate_b, gate_c, x_in = [jnp.concatenate(tb, axis=0) for tb in zip(tops, bots)]
            v = gate_c * x_in
            conv = pltpu.roll(v, 2, 0) * cw_ref[0:1, :]
            conv = conv + pltpu.roll(v, 1, 0) * cw_ref[1:2, :]
            conv = conv + v * cw_ref[2:3, :]
            accumulate((gate_b * conv)[HALO:, :].astype(BF16))

        @pl.when(jnp.logical_and(pool_step >= 0, pool_step % 2 == 0))
        def _():
            hb = xb_ref[(i - 1) % 2]
            z = jnp.concatenate([jnp.dot(hb, wa_ref[...], preferred_element_type=F32),
                                 jnp.dot(hb, wb_ref[...], preferred_element_type=F32)], axis=1)
            s = z
            for level in range(len(POOL_WINDOWS)):
                doubled = s + pltpu.roll(s, 1 << level, 0)
                s = doubled if level == 0 else jnp.where(level <= group, doubled, s)
            inv_w = jnp.where(group == 0, 1.0 / POOL_WINDOWS[0],
                              jnp.where(group == 1, 1.0 / POOL_WINDOWS[1],
                                        jnp.where(group == 2, 1.0 / POOL_WINDOWS[2],
                                                  1.0 / POOL_WINDOWS[3])))
            diff = (s * inv_w.astype(F32) - z)[HALO:, :].astype(BF16)
            y = (jnp.dot(diff, pw_ref[...], preferred_element_type=F32) * ps_ref[...]).astype(BF16)
            yp_ref[...] = y
            accumulate(y[:, 0:tc])

        @pl.when(jnp.logical_and(pool_step >= 0, pool_step % 2 == 1))
        def _():
            accumulate(yp_ref[:, tc:])

        @pl.when(c == pl.num_programs(1) - 1)
        def _():
            _layer_norm_and_write_back(acc_ref, out_hbm, sem, (i - 1) * ts, None, g_ref, b_ref)


def _mixer_ln(h, meta_h, w_in, conv_w, pool_w, pool_scale, w_out, ln_g, ln_b, *, ts, seq):
    m, d = h.shape
    cc = conv_w.shape[1]
    n_groups, pg, _ = pool_w.shape
    n_panels, _, tc = w_in.shape
    n_conv = cc // tc
    assert pg == 2 * tc and n_panels == 3 * n_conv + 2 * n_groups and w_out.shape[0] == cc + n_groups * pg
    n_steps = n_conv + 2 * n_groups
    n_tiles = m // ts
    tiles_per_seq = seq // ts
    n_slabs = d // MIXER_SLAB_COLS
    assert m % ts == 0 and seq % ts == 0 and n_slabs <= n_steps and ts % _ln_group_rows(ts) == 0

    def step(i, c):
        return jnp.where(i == 0, 0, c)

    def group(i, c):
        return jnp.maximum(step(i, c) - n_conv, 0) // 2

    def slab_col(c):
        return jnp.minimum(c, n_slabs - 1)

    def next_tile(i):
        return jnp.minimum(i, n_tiles - 1)

    def pool_or(i, c, conv_panel, pool_panel):
        return jnp.where(step(i, c) < n_conv, conv_panel, pool_panel)

    return pl.pallas_call(
        functools.partial(
            _mixer_kernel, n_tiles=n_tiles, ts=ts, tiles_per_seq=tiles_per_seq, n_conv=n_conv),
        grid=(n_tiles + 1, n_steps),
        in_specs=[
            pl.BlockSpec((ts, MIXER_SLAB_COLS), lambda i, c: (jnp.maximum(i - 1, 0), slab_col(c))),
            pl.BlockSpec((ts, MIXER_SLAB_COLS), lambda i, c: (next_tile(i), slab_col(c))),
            pl.BlockSpec((HALO, MIXER_SLAB_COLS),
                         lambda i, c: (jnp.maximum(next_tile(i) * (ts // HALO) - 1, 0), slab_col(c))),
            pl.BlockSpec((HALO, MIXER_SLAB_COLS), lambda i, c: (0, slab_col(c))),
            pl.BlockSpec((None, d, tc), lambda i, c: (
                pool_or(i, c, step(i, c), 3 * n_conv + 2 * group(i, c)), 0, 0)),
            pl.BlockSpec((None, d, tc), lambda i, c: (
                pool_or(i, c, n_conv + step(i, c), 3 * n_conv + 2 * group(i, c) + 1), 0, 0)),
            pl.BlockSpec((None, d, tc), lambda i, c: (
                2 * n_conv + jnp.minimum(step(i, c), n_conv - 1), 0, 0)),
            pl.BlockSpec((tc, d), lambda i, c: (step(i, c), 0)),
            pl.BlockSpec((None, pg, pg), lambda i, c: (group(i, c), 0, 0)),
            pl.BlockSpec((CONV_K, tc), lambda i, c: (0, jnp.minimum(step(i, c), n_conv - 1))),
            pl.BlockSpec((1, pg), lambda i, c: (0, group(i, c))),
            pl.BlockSpec((1, d), lambda i, c: (0, 0)),
            pl.BlockSpec((1, d), lambda i, c: (0, 0)),
        ],
        out_specs=pl.BlockSpec(memory_space=pl.ANY),
        out_shape=jax.ShapeDtypeStruct((m, d), F32),
        scratch_shapes=[
            pltpu.VMEM((ts, d), F32),
            pltpu.VMEM((2, HALO + ts, d), BF16),
            pltpu.VMEM((ts, pg), BF16),
            pltpu.SemaphoreType.DMA((OUT_COPIES_IN_FLIGHT,)),
        ],
        compiler_params=pltpu.CompilerParams(
            dimension_semantics=("arbitrary", "arbitrary"),
            vmem_limit_bytes=V7X_VMEM_LIMIT_BYTES,
        ),
        name="mixer_ln",
    )(h, h, h, meta_h, w_in, w_in, w_in, w_out, pool_w, conv_w, pool_scale, ln_g, ln_b)


def kernel(x, meta_tokens, ffn1_w_gu, ffn1_w_down, ln1_g, ln1_b, w_in, conv_w, pool_w, pool_scale,
           w_out, ln2_g, ln2_b, ffn2_w_gu, ffn2_w_down, ln3_g, ln3_b):
    b, seq, d = x.shape
    assert meta_tokens.shape == (N_META, d) and N_META == HALO
    assert ffn1_w_gu.shape[0] == DEPTH

    h = x.reshape(b * seq, d)
    meta = meta_tokens.astype(x.dtype)
    for i in range(DEPTH):
        g1, b1 = ln1_g[i][None].astype(F32), ln1_b[i][None].astype(F32)
        g2, b2 = ln2_g[i][None].astype(F32), ln2_b[i][None].astype(F32)
        g3, b3 = ln3_g[i][None].astype(F32), ln3_b[i][None].astype(F32)
        n_groups, pg, _ = pool_w[i].shape

        h1_head, meta1, wgu1, wd1 = _ffn_ln_head(
            h, meta, ffn1_w_gu[i], ffn1_w_down[i], g1, b1, tm=HEAD_ROWS)
        jobs = (
            _CastJob(ffn2_w_gu[i], (32, ffn2_w_gu[i].shape[1] // 2), 0, panel=V7X_MXU_COLS),
            _CastJob(ffn2_w_down[i], (64, d), 0),
            _CastJob(w_out[i], (64, d), ffn2_w_down[i].shape[0] // 64),
            _CastJob(w_in[i], (32, w_in[i].shape[1]), 0, panel=V7X_MXU_COLS),
            _CastJob(pool_w[i].reshape(n_groups * pg, pg), (128, pg), w_in[i].shape[0] // 32),
        )
        h1, (wgu2, wd2, w_out_b, w_in_b, pool_w_b) = _ffn_ln(
            h, wgu1, wd1, g1, b1, tm=FFN1_TILE_ROWS, tail=h1_head, cast_jobs=jobs)
        pool_w_b = pool_w_b.reshape(n_groups, pg, pg)

        h2 = _mixer_ln(h1, meta1, w_in_b, conv_w[i], pool_w_b, pool_scale[i][None], w_out_b, g2, b2,
                       ts=MIXER_TILE_ROWS, seq=seq)
        h, _ = _ffn_ln(h2, wgu2, wd2, g3, b3, tm=FFN2_TILE_ROWS)
        assert DEPTH == 1
    return h.reshape(b, seq, d)
```

```python
import functools
from typing import NamedTuple

import jax
import jax.numpy as jnp
from jax import lax
from jax.experimental import pallas as pl
from jax.experimental.pallas import tpu as pltpu

N_META = 16
CONV_K = 3
POOL_WINDOWS = (2, 4, 8, 16)
LN_EPS = 1e-5
DEPTH = 1
ALPHA = (2.0 * DEPTH) ** 0.25

BF16 = jnp.bfloat16
F32 = jnp.float32

V7X_MXU_COLS = 256
V7X_VMEM_LIMIT_BYTES = 60 * 1024 * 1024
HALO = 16
LN_ROWS = 8
LN_UNROLL = 16
FFN_DOWN_COLS = 512
X_SLAB_COLS = 128
MIXER_SLAB_COLS = 256
OUT_COPIES_IN_FLIGHT = 4

HEAD_ROWS = 512
FFN1_TILE_ROWS = 768
FFN2_TILE_ROWS = 1024
MIXER_TILE_ROWS = 1024


def _layer_norm_rows(y, g, b):
    mu = jnp.mean(y, axis=-1, keepdims=True)
    yc = y - mu
    var = jnp.mean(yc * yc, axis=-1, keepdims=True)
    return yc * lax.rsqrt(var + LN_EPS) * g + b


def _ln_group_rows(rows):
    return min(LN_ROWS * LN_UNROLL, rows)


def _layer_norm_group(o_ref, r0, res_ref, branch_scale, g_ref, b_ref):
    group = _ln_group_rows(o_ref.shape[0])
    outs = []
    for u in range(0, group, LN_ROWS):
        sub = pl.ds(r0 + u, LN_ROWS)
        y = o_ref[sub, :]
        if branch_scale is not None:
            y = branch_scale * y
        if res_ref is not None:
            y = ALPHA * res_ref[sub, :] + y
        outs.append(_layer_norm_rows(y, g_ref[...], b_ref[...]))
    for k, u in enumerate(range(0, group, LN_ROWS)):
        o_ref[pl.ds(r0 + u, LN_ROWS), :] = outs[k]


def _residual_layer_norm(o_ref, res_ref, branch_scale, g_ref, b_ref):
    rows = o_ref.shape[0]
    group = _ln_group_rows(rows)

    def body(r, carry):
        _layer_norm_group(o_ref, pl.multiple_of(r * group, group), res_ref, branch_scale, g_ref, b_ref)
        return carry

    lax.fori_loop(0, rows // group, body, 0)


def _swiglu_chunk(xb, wgu):
    tf = wgu.shape[1] // 2
    gu = jnp.dot(xb, wgu, preferred_element_type=F32)
    gate, up = gu[:, :tf], gu[:, tf:]
    return (gate * jax.nn.sigmoid(gate) * up).astype(BF16)


class _CastJob(NamedTuple):
    src: jax.Array
    block: tuple
    start: int
    panel: int = 0

    @property
    def n_blocks(self):
        return (self.src.shape[0] // self.block[0]) * (self.src.shape[1] // self.block[1])

    @property
    def out_shape(self):
        rows, cols = self.src.shape
        col_blocks = cols // self.block[1]
        shape = (self.block[1] // self.panel, rows, col_blocks * self.panel) if self.panel else (rows, cols)
        return jax.ShapeDtypeStruct(shape, BF16)

    def _block_index(self, steps_per_row_tile, i, j):
        col_blocks = self.src.shape[1] // self.block[1]
        s = jnp.clip(i * steps_per_row_tile + j - self.start, 0, self.n_blocks - 1)
        return s // col_blocks, s % col_blocks

    def in_spec(self, steps_per_row_tile):
        return pl.BlockSpec(self.block, functools.partial(self._block_index, steps_per_row_tile))

    def out_spec(self, steps_per_row_tile):
        if not self.panel:
            return self.in_spec(steps_per_row_tile)
        br, bc = self.block

        def index_map(i, j):
            rb, cb = self._block_index(steps_per_row_tile, i, j)
            return (0, rb, cb)

        return pl.BlockSpec((bc // self.panel, br, self.panel), index_map)


def _cast_block(src_ref, dst_ref):
    if len(dst_ref.shape) == 2:
        dst_ref[...] = src_ref[...].astype(BF16)
    else:
        panel = dst_ref.shape[2]
        for p in range(dst_ref.shape[0]):
            dst_ref[p] = src_ref[:, p * panel:(p + 1) * panel].astype(BF16)


def _ffn_kernel(*refs, n_tiles, tm, cast_ranges):
    n_cast = len(cast_ranges)
    n_fixed = 6
    xres_ref, xnext_ref, wgu_ref, wd_ref, g_ref, b_ref = refs[:n_fixed]
    cast_in = refs[n_fixed:n_fixed + n_cast]
    n_in = len(refs) - (5 + n_cast)
    tail_hbm = refs[n_fixed + n_cast] if n_in > n_fixed + n_cast else None
    out_hbm = refs[n_in]
    cast_out = refs[n_in + 1:n_in + 1 + n_cast]
    acc_ref, xb_ref, sem, tail_sem = refs[n_in + 1 + n_cast:]
    i = pl.program_id(0)
    j = pl.program_id(1)
    nf = pl.num_programs(1)
    d = acc_ref.shape[1]
    n_slabs = d // X_SLAB_COLS
    slab = pl.ds(pl.multiple_of(jnp.minimum(j, n_slabs - 1) * X_SLAB_COLS, X_SLAB_COLS), X_SLAB_COLS)

    if tail_hbm is not None:
        tail_copy = pltpu.make_async_copy(
            tail_hbm, out_hbm.at[pl.ds(n_tiles * tm, tail_hbm.shape[0])], tail_sem)

        @pl.when(jnp.logical_and(i == 0, j == 0))
        def _():
            tail_copy.start()

        @pl.when(jnp.logical_and(i == n_tiles, j == nf - 1))
        def _():
            tail_copy.wait()

    @pl.when(jnp.logical_and(i < n_tiles, j < n_slabs))
    def _():
        xb_ref.at[i % 2][:, slab] = xnext_ref[...].astype(BF16)

    step = (i - 1) * nf + j
    for src_ref, dst_ref, (start, stop) in zip(cast_in, cast_out, cast_ranges):
        @pl.when(jnp.logical_and(step >= start, step < stop))
        def _(src_ref=src_ref, dst_ref=dst_ref):
            _cast_block(src_ref, dst_ref)

    @pl.when(i > 0)
    def _():
        @pl.when(j == 0)
        def _():
            acc_ref[...] = jnp.zeros_like(acc_ref)

        @pl.when(j < n_slabs)
        def _():
            acc_ref[:, slab] += (ALPHA / 0.5) * xres_ref[...]

        act = _swiglu_chunk(xb_ref[(i - 1) % 2], wgu_ref[...])
        for n0 in range(0, d, FFN_DOWN_COLS):
            cols = slice(n0, n0 + FFN_DOWN_COLS)
            acc_ref[:, cols] += jnp.dot(act, wd_ref[:, cols], preferred_element_type=F32)

        @pl.when(j == nf - 1)
        def _():
            _layer_norm_and_write_back(acc_ref, out_hbm, sem, (i - 1) * tm, 0.5, g_ref, b_ref)


def _layer_norm_and_write_back(acc_ref, out_hbm, sem, row0, branch_scale, g_ref, b_ref):
    rows = acc_ref.shape[0]
    group = _ln_group_rows(rows)
    n_groups = rows // group
    in_flight = min(OUT_COPIES_IN_FLIGHT, n_groups)

    def group_copy(k):
        r0 = pl.multiple_of(k * group, group)
        return pltpu.make_async_copy(
            acc_ref.at[pl.ds(r0, group)], out_hbm.at[pl.ds(row0 + r0, group)], sem.at[k % in_flight])

    def body(k, carry):
        @pl.when(k >= in_flight)
        def _():
            group_copy(k - in_flight).wait()

        _layer_norm_group(acc_ref, pl.multiple_of(k * group, group), None, branch_scale, g_ref, b_ref)
        group_copy(k).start()
        return carry

    lax.fori_loop(0, n_groups, body, 0)
    for k in range(n_groups - in_flight, n_groups):
        group_copy(k).wait()


def _ffn_ln(x, wgu, w_down, ln_g, ln_b, *, tm, tail=None, cast_jobs=()):
    m, d = x.shape
    f = w_down.shape[0]
    tf = wgu.shape[2] // 2
    n_tiles = (m - (0 if tail is None else tail.shape[0])) // tm
    assert n_tiles * tm + (0 if tail is None else tail.shape[0]) == m
    assert f % tf == 0 and tf == V7X_MXU_COLS and tm % _ln_group_rows(tm) == 0
    nf = f // tf
    n_slabs = d // X_SLAB_COLS
    assert n_slabs <= nf
    for job in cast_jobs:
        assert job.start + job.n_blocks <= n_tiles * nf

    def weight_chunk(i, j):
        return jnp.where(i == 0, 0, j)

    def slab_col(j):
        return jnp.minimum(j, n_slabs - 1)

    shifted_jobs = [job._replace(start=job.start + nf) for job in cast_jobs]
    operands = [x, x, wgu, w_down, ln_g, ln_b] + [job.src for job in cast_jobs]
    if tail is not None:
        operands.append(tail)
    outs = pl.pallas_call(
        functools.partial(
            _ffn_kernel, n_tiles=n_tiles, tm=tm,
            cast_ranges=tuple((job.start, job.start + job.n_blocks) for job in cast_jobs)),
        grid=(n_tiles + 1, nf),
        in_specs=[
            pl.BlockSpec((tm, X_SLAB_COLS), lambda i, j: (jnp.maximum(i - 1, 0), slab_col(j))),
            pl.BlockSpec((tm, X_SLAB_COLS), lambda i, j: (jnp.minimum(i, n_tiles - 1), slab_col(j))),
            pl.BlockSpec((None, d, 2 * tf), lambda i, j: (weight_chunk(i, j), 0, 0)),
            pl.BlockSpec((tf, d), lambda i, j: (weight_chunk(i, j), 0)),
            pl.BlockSpec((1, d), lambda i, j: (0, 0)),
            pl.BlockSpec((1, d), lambda i, j: (0, 0)),
        ] + [job.in_spec(nf) for job in shifted_jobs]
        + ([pl.BlockSpec(memory_space=pl.ANY)] if tail is not None else []),
        out_specs=[pl.BlockSpec(memory_space=pl.ANY)] + [job.out_spec(nf) for job in shifted_jobs],
        out_shape=[jax.ShapeDtypeStruct((m, d), F32)] + [job.out_shape for job in cast_jobs],
        scratch_shapes=[
            pltpu.VMEM((tm, d), F32),
            pltpu.VMEM((2, tm, d), BF16),
            pltpu.SemaphoreType.DMA((OUT_COPIES_IN_FLIGHT,)),
            pltpu.SemaphoreType.DMA(()),
        ],
        compiler_params=pltpu.CompilerParams(
            dimension_semantics=("arbitrary", "arbitrary"),
            vmem_limit_bytes=V7X_VMEM_LIMIT_BYTES,
        ),
        name="ffn_ln",
    )(*operands)
    return outs[0], tuple(outs[1:])


def _ffn_nested_kernel(x_hbm, wgu_hbm, wd_hbm, g_hbm, b_hbm, out_hbm, acc_ref, xb_ref, sem,
                       *, n_tiles, tm, nf):
    d = acc_ref.shape[1]
    tf = wd_hbm.shape[0] // nf
    n_slabs = d // X_SLAB_COLS

    def step(xres_ref, xnext_ref, wgu_ref, wd_ref, g_ref, b_ref):
        i = pl.program_id(0)
        j = pl.program_id(1)
        slab = pl.ds(
            pl.multiple_of(jnp.minimum(j, n_slabs - 1) * X_SLAB_COLS, X_SLAB_COLS), X_SLAB_COLS)

        @pl.when(jnp.logical_and(i < n_tiles, j < n_slabs))
        def _():
            xb_ref.at[i % 2][:, slab] = xnext_ref[...].astype(BF16)

        @pl.when(i > 0)
        def _():
            @pl.when(j == 0)
            def _():
                acc_ref[...] = jnp.zeros_like(acc_ref)

            @pl.when(j < n_slabs)
            def _():
                acc_ref[:, slab] += (ALPHA / 0.5) * xres_ref[...]

            act = _swiglu_chunk(xb_ref[(i - 1) % 2], wgu_ref[...])
            for n0 in range(0, d, FFN_DOWN_COLS):
                cols = slice(n0, n0 + FFN_DOWN_COLS)
                acc_ref[:, cols] += jnp.dot(act, wd_ref[:, cols], preferred_element_type=F32)

            @pl.when(j == nf - 1)
            def _():
                _layer_norm_and_write_back(acc_ref, out_hbm, sem, (i - 1) * tm, 0.5, g_ref, b_ref)

    def weight_chunk(i, j):
        return jnp.where(i == 0, 0, j)

    def slab_col(j):
        return jnp.minimum(j, n_slabs - 1)

    pltpu.emit_pipeline(
        step,
        grid=(n_tiles + 1, nf),
        in_specs=[
            pl.BlockSpec((tm, X_SLAB_COLS), lambda i, j: (jnp.maximum(i - 1, 0), slab_col(j))),
            pl.BlockSpec((tm, X_SLAB_COLS), lambda i, j: (jnp.minimum(i, n_tiles - 1), slab_col(j))),
            pl.BlockSpec((None, d, 2 * tf), lambda i, j: (weight_chunk(i, j), 0, 0)),
            pl.BlockSpec((tf, d), lambda i, j: (weight_chunk(i, j), 0)),
            pl.BlockSpec((1, d), lambda i, j: (0, 0)),
            pl.BlockSpec((1, d), lambda i, j: (0, 0)),
        ],
    )(x_hbm, x_hbm, wgu_hbm, wd_hbm, g_hbm, b_hbm)


def _ffn_ln_nested(x, wgu, w_down, ln_g, ln_b, *, tm):
    m, d = x.shape
    f = w_down.shape[0]
    tf = wgu.shape[2] // 2
    assert m % tm == 0 and f % tf == 0 and tf == V7X_MXU_COLS and tm % _ln_group_rows(tm) == 0
    nf = f // tf
    assert d // X_SLAB_COLS <= nf
    any_space = pl.BlockSpec(memory_space=pl.ANY)
    return pl.pallas_call(
        functools.partial(_ffn_nested_kernel, n_tiles=m // tm, tm=tm, nf=nf),
        in_specs=[any_space] * 5,
        out_specs=any_space,
        out_shape=jax.ShapeDtypeStruct((m, d), F32),
        scratch_shapes=[
            pltpu.VMEM((tm, d), F32),
            pltpu.VMEM((2, tm, d), BF16),
            pltpu.SemaphoreType.DMA((OUT_COPIES_IN_FLIGHT,)),
        ],
        compiler_params=pltpu.CompilerParams(vmem_limit_bytes=V7X_VMEM_LIMIT_BYTES),
        name="ffn_ln_nested",
    )(x, wgu, w_down, ln_g, ln_b)


def _ffn_head_kernel(x_ref, meta_ref, wg_ref, wu_ref, wd_ref, g_ref, b_ref,
                     o_ref, om_ref, wgub_ref, wdb_ref, xb_ref):
    j = pl.program_id(0)
    n_meta = meta_ref.shape[0]

    @pl.when(j == 0)
    def _():
        xb_ref[0:n_meta, :] = meta_ref[...].astype(BF16)
        xb_ref[n_meta:, :] = x_ref[...].astype(BF16)
        o_ref[...] = jnp.zeros_like(o_ref)
        om_ref[...] = jnp.zeros_like(om_ref)

    tf = wg_ref.shape[1]
    wgub_ref[:, 0:tf] = wg_ref[...].astype(BF16)
    wgub_ref[:, tf:] = wu_ref[...].astype(BF16)
    wdb_ref[...] = wd_ref[...].astype(BF16)
    act = _swiglu_chunk(xb_ref[...], wgub_ref[...])
    d = o_ref.shape[1]
    for n0 in range(0, d, FFN_DOWN_COLS):
        cols = slice(n0, n0 + FFN_DOWN_COLS)
        part = jnp.dot(act, wdb_ref[:, cols], preferred_element_type=F32)
        om_ref[:, cols] += part[0:n_meta, :]
        o_ref[:, cols] += part[n_meta:, :]

    @pl.when(j == pl.num_programs(0) - 1)
    def _():
        _residual_layer_norm(om_ref, meta_ref, 0.5, g_ref, b_ref)
        _residual_layer_norm(o_ref, x_ref, 0.5, g_ref, b_ref)


def _ffn_ln_head(x, meta, w_gu, w_down, ln_g, ln_b, *, tm):
    m, d = x.shape
    n_meta = meta.shape[0]
    f = w_down.shape[0]
    tf = V7X_MXU_COLS
    nf = f // tf
    assert m % tm == 0
    last = m // tm - 1
    resident = pl.Buffered(1)
    return pl.pallas_call(
        _ffn_head_kernel,
        grid=(nf,),
        in_specs=[
            pl.BlockSpec((tm, d), lambda j: (last, 0), pipeline_mode=resident),
            pl.BlockSpec((n_meta, d), lambda j: (0, 0), pipeline_mode=resident),
            pl.BlockSpec((d, tf), lambda j: (0, j)),
            pl.BlockSpec((d, tf), lambda j: (0, nf + j)),
            pl.BlockSpec((tf, d), lambda j: (j, 0)),
            pl.BlockSpec((1, d), lambda j: (0, 0)),
            pl.BlockSpec((1, d), lambda j: (0, 0)),
        ],
        out_specs=[
            pl.BlockSpec((tm, d), lambda j: (0, 0), pipeline_mode=resident),
            pl.BlockSpec((n_meta, d), lambda j: (0, 0), pipeline_mode=resident),
            pl.BlockSpec((None, d, 2 * tf), lambda j: (j, 0, 0)),
            pl.BlockSpec((tf, d), lambda j: (j, 0)),
        ],
        out_shape=[
            jax.ShapeDtypeStruct((tm, d), F32),
            jax.ShapeDtypeStruct((n_meta, d), F32),
            jax.ShapeDtypeStruct((nf, d, 2 * tf), BF16),
            jax.ShapeDtypeStruct((f, d), BF16),
        ],
        scratch_shapes=[pltpu.VMEM((n_meta + tm, d), BF16)],
        compiler_params=pltpu.CompilerParams(
            dimension_semantics=("arbitrary",),
            vmem_limit_bytes=V7X_VMEM_LIMIT_BYTES,
        ),
        name="ffn_ln_head",
    )(x, meta, w_gu, w_gu, w_down, ln_g, ln_b)


def _mixer_kernel(hres_ref, hnext_ref, halo_ref, meta_ref, wa_ref, wb_ref, wc_ref, wo_ref, pw_ref,
                  cw_ref, ps_ref, g_ref, b_ref, out_hbm, acc_ref, xb_ref, yp_ref, sem,
                  *, n_tiles, ts, tiles_per_seq, n_conv):
    i = pl.program_id(0)
    c = pl.program_id(1)
    d = acc_ref.shape[1]
    tc = wa_ref.shape[1]
    n_slabs = d // MIXER_SLAB_COLS
    slab = pl.ds(
        pl.multiple_of(jnp.minimum(c, n_slabs - 1) * MIXER_SLAB_COLS, MIXER_SLAB_COLS), MIXER_SLAB_COLS)

    def round_next_slab():
        first = (i % tiles_per_seq) == 0
        nxt = xb_ref.at[i % 2]
        nxt[0:HALO, slab] = jnp.where(first, meta_ref[...], halo_ref[...]).astype(BF16)
        nxt[HALO:, slab] = hnext_ref[...].astype(BF16)

    def accumulate(y):
        for n0 in range(0, d, FFN_DOWN_COLS):
            cols = slice(n0, n0 + FFN_DOWN_COLS)
            acc_ref[:, cols] += jnp.dot(y, wo_ref[:, cols], preferred_element_type=F32)

    @pl.when(jnp.logical_and(i < n_tiles, c < n_slabs))
    def _():
        round_next_slab()

    @pl.when(i > 0)
    def _():
        @pl.when(c == 0)
        def _():
            acc_ref[...] = jnp.zeros_like(acc_ref)

        @pl.when(c < n_slabs)
        def _():
            acc_ref[:, slab] += ALPHA * hres_ref[...]

        pool_step = c - n_conv
        group = pool_step // 2

        @pl.when(c < n_conv)
        def _():
            half = HALO + (ts // 2)
            hb_ref = xb_ref.at[(i - 1) % 2]
            tops = [jnp.dot(hb_ref[0:half, :], w[...], preferred_element_type=F32)
                    for w in (wa_ref, wb_ref, wc_ref)]
            bots = [jnp.dot(hb_ref[half:, :], w[...], preferred_element_type=F32)
                    for w in (wa_ref, wb_ref, wc_ref)]
            gate_b, gate_c, x_in = [jnp.concatenate(tb, axis=0) for tb in zip(tops, bots)]
            v = gate_c * x_in
            conv = pltpu.roll(v, 2, 0) * cw_ref[0:1, :]
            conv = conv + pltpu.roll(v, 1, 0) * cw_ref[1:2, :]
            conv = conv + v * cw_ref[2:3, :]
            accumulate((gate_b * conv)[HALO:, :].astype(BF16))

        @pl.when(jnp.logical_and(pool_step >= 0, pool_step % 2 == 0))
        def _():
            hb = xb_ref[(i - 1) % 2]
            z = jnp.concatenate([jnp.dot(hb, wa_ref[...], preferred_element_type=F32),
                                 jnp.dot(hb, wb_ref[...], preferred_element_type=F32)], axis=1)
            s = z
            for level in range(len(POOL_WINDOWS)):
                doubled = s + pltpu.roll(s, 1 << level, 0)
                s = doubled if level == 0 else jnp.where(level <= group, doubled, s)
            inv_w = jnp.where(group == 0, 1.0 / POOL_WINDOWS[0],
                              jnp.where(group == 1, 1.0 / POOL_WINDOWS[1],
                                        jnp.where(group == 2, 1.0 / POOL_WINDOWS[2],
                                                  1.0 / POOL_WINDOWS[3])))
            diff = (s * inv_w.astype(F32) - z)[HALO:, :].astype(BF16)
            y = (jnp.dot(diff, pw_ref[...], preferred_element_type=F32) * ps_ref[...]).astype(BF16)
            yp_ref[...] = y
            accumulate(y[:, 0:tc])

        @pl.when(jnp.logical_and(pool_step >= 0, pool_step % 2 == 1))
        def _():
            accumulate(yp_ref[:, tc:])

        @pl.when(c == pl.num_programs(1) - 1)
        def _():
            _layer_norm_and_write_back(acc_ref, out_hbm, sem, (i - 1) * ts, None, g_ref, b_ref)


def _mixer_ln(h, meta_h, w_in, conv_w, pool_w, pool_scale, w_out, ln_g, ln_b, *, ts, seq):
    m, d = h.shape
    cc = conv_w.shape[1]
    n_groups, pg, _ = pool_w.shape
    n_panels, _, tc = w_in.shape
    n_conv = cc // tc
    assert pg == 2 * tc and n_panels == 3 * n_conv + 2 * n_groups and w_out.shape[0] == cc + n_groups * pg
    n_steps = n_conv + 2 * n_groups
    n_tiles = m // ts
    tiles_per_seq = seq // ts
    n_slabs = d // MIXER_SLAB_COLS
    assert m % ts == 0 and seq % ts == 0 and n_slabs <= n_steps and ts % _ln_group_rows(ts) == 0

    def step(i, c):
        return jnp.where(i == 0, 0, c)

    def group(i, c):
        return jnp.maximum(step(i, c) - n_conv, 0) // 2

    def slab_col(c):
        return jnp.minimum(c, n_slabs - 1)

    def next_tile(i):
        return jnp.minimum(i, n_tiles - 1)

    def pool_or(i, c, conv_panel, pool_panel):
        return jnp.where(step(i, c) < n_conv, conv_panel, pool_panel)

    return pl.pallas_call(
        functools.partial(
            _mixer_kernel, n_tiles=n_tiles, ts=ts, tiles_per_seq=tiles_per_seq, n_conv=n_conv),
        grid=(n_tiles + 1, n_steps),
        in_specs=[
            pl.BlockSpec((ts, MIXER_SLAB_COLS), lambda i, c: (jnp.maximum(i - 1, 0), slab_col(c))),
            pl.BlockSpec((ts, MIXER_SLAB_COLS), lambda i, c: (next_tile(i), slab_col(c))),
            pl.BlockSpec((HALO, MIXER_SLAB_COLS),
                         lambda i, c: (jnp.maximum(next_tile(i) * (ts // HALO) - 1, 0), slab_col(c))),
            pl.BlockSpec((HALO, MIXER_SLAB_COLS), lambda i, c: (0, slab_col(c))),
            pl.BlockSpec((None, d, tc), lambda i, c: (
                pool_or(i, c, step(i, c), 3 * n_conv + 2 * group(i, c)), 0, 0)),
            pl.BlockSpec((None, d, tc), lambda i, c: (
                pool_or(i, c, n_conv + step(i, c), 3 * n_conv + 2 * group(i, c) + 1), 0, 0)),
            pl.BlockSpec((None, d, tc), lambda i, c: (
                2 * n_conv + jnp.minimum(step(i, c), n_conv - 1), 0, 0)),
            pl.BlockSpec((tc, d), lambda i, c: (step(i, c), 0)),
            pl.BlockSpec((None, pg, pg), lambda i, c: (group(i, c), 0, 0)),
            pl.BlockSpec((CONV_K, tc), lambda i, c: (0, jnp.minimum(step(i, c), n_conv - 1))),
            pl.BlockSpec((1, pg), lambda i, c: (0, group(i, c))),
            pl.BlockSpec((1, d), lambda i, c: (0, 0)),
            pl.BlockSpec((1, d), lambda i, c: (0, 0)),
        ],
        out_specs=pl.BlockSpec(memory_space=pl.ANY),
        out_shape=jax.ShapeDtypeStruct((m, d), F32),
        scratch_shapes=[
            pltpu.VMEM((ts, d), F32),
            pltpu.VMEM((2, HALO + ts, d), BF16),
            pltpu.VMEM((ts, pg), BF16),
            pltpu.SemaphoreType.DMA((OUT_COPIES_IN_FLIGHT,)),
        ],
        compiler_params=pltpu.CompilerParams(
            dimension_semantics=("arbitrary", "arbitrary"),
            vmem_limit_bytes=V7X_VMEM_LIMIT_BYTES,
        ),
        name="mixer_ln",
    )(h, h, h, meta_h, w_in, w_in, w_in, w_out, pool_w, conv_w, pool_scale, ln_g, ln_b)


def kernel(x, meta_tokens, ffn1_w_gu, ffn1_w_down, ln1_g, ln1_b, w_in, conv_w, pool_w, pool_scale,
           w_out, ln2_g, ln2_b, ffn2_w_gu, ffn2_w_down, ln3_g, ln3_b):
    b, seq, d = x.shape
    assert meta_tokens.shape == (N_META, d) and N_META == HALO
    assert ffn1_w_gu.shape[0] == DEPTH

    h = x.reshape(b * seq, d)
    meta = meta_tokens.astype(x.dtype)
    for i in range(DEPTH):
        g1, b1 = ln1_g[i][None].astype(F32), ln1_b[i][None].astype(F32)
        g2, b2 = ln2_g[i][None].astype(F32), ln2_b[i][None].astype(F32)
        g3, b3 = ln3_g[i][None].astype(F32), ln3_b[i][None].astype(F32)
        n_groups, pg, _ = pool_w[i].shape

        h1_head, meta1, wgu1, wd1 = _ffn_ln_head(
            h, meta, ffn1_w_gu[i], ffn1_w_down[i], g1, b1, tm=HEAD_ROWS)
        jobs = (
            _CastJob(ffn2_w_gu[i], (32, ffn2_w_gu[i].shape[1] // 2), 0, panel=V7X_MXU_COLS),
            _CastJob(ffn2_w_down[i], (64, d), 0),
            _CastJob(w_out[i], (64, d), ffn2_w_down[i].shape[0] // 64),
            _CastJob(w_in[i], (32, w_in[i].shape[1]), 0, panel=V7X_MXU_COLS),
            _CastJob(pool_w[i].reshape(n_groups * pg, pg), (128, pg), w_in[i].shape[0] // 32),
        )
        h1, (wgu2, wd2, w_out_b, w_in_b, pool_w_b) = _ffn_ln(
            h, wgu1, wd1, g1, b1, tm=FFN1_TILE_ROWS, tail=h1_head, cast_jobs=jobs)
        pool_w_b = pool_w_b.reshape(n_groups, pg, pg)

        h2 = _mixer_ln(h1, meta1, w_in_b, conv_w[i], pool_w_b, pool_scale[i][None], w_out_b, g2, b2,
                       ts=MIXER_TILE_ROWS, seq=seq)
        h = _ffn_ln_nested(h2, wgu2, wd2, g3, b3, tm=FFN2_TILE_ROWS)
        assert DEPTH == 1
    return h.reshape(b, seq, d)
```

```python
import functools
from typing import NamedTuple

import jax
import jax.numpy as jnp
from jax import lax
from jax.experimental import pallas as pl
from jax.experimental.pallas import tpu as pltpu

N_META = 16
CONV_K = 3
POOL_WINDOWS = (2, 4, 8, 16)
LN_EPS = 1e-5
DEPTH = 1
ALPHA = (2.0 * DEPTH) ** 0.25

BF16 = jnp.bfloat16
F32 = jnp.float32

V7X_MXU_COLS = 256
V7X_VMEM_LIMIT_BYTES = 60 * 1024 * 1024
HALO = 16
LN_ROWS = 8
LN_UNROLL = 16
FFN_DOWN_COLS = 512
X_SLAB_COLS = 128
MIXER_SLAB_COLS = 256
OUT_COPIES_IN_FLIGHT = 4

HEAD_ROWS = 512
FFN1_TILE_ROWS = 768
FFN2_TILE_ROWS = 1024
MIXER_TILE_ROWS = 1024


def _layer_norm_rows(y, g, b):
    mu = jnp.mean(y, axis=-1, keepdims=True)
    yc = y - mu
    var = jnp.mean(yc * yc, axis=-1, keepdims=True)
    return yc * lax.rsqrt(var + LN_EPS) * g + b


def _ln_group_rows(rows):
    return min(LN_ROWS * LN_UNROLL, rows)


def _layer_norm_group(o_ref, r0, res_ref, branch_scale, g_ref, b_ref):
    group = _ln_group_rows(o_ref.shape[0])
    outs = []
    for u in range(0, group, LN_ROWS):
        sub = pl.ds(r0 + u, LN_ROWS)
        y = o_ref[sub, :]
        if branch_scale is not None:
            y = branch_scale * y
        if res_ref is not None:
            y = ALPHA * res_ref[sub, :] + y
        outs.append(_layer_norm_rows(y, g_ref[...], b_ref[...]))
    for k, u in enumerate(range(0, group, LN_ROWS)):
        o_ref[pl.ds(r0 + u, LN_ROWS), :] = outs[k]


def _residual_layer_norm(o_ref, res_ref, branch_scale, g_ref, b_ref):
    rows = o_ref.shape[0]
    group = _ln_group_rows(rows)

    def body(r, carry):
        _layer_norm_group(o_ref, pl.multiple_of(r * group, group), res_ref, branch_scale, g_ref, b_ref)
        return carry

    lax.fori_loop(0, rows // group, body, 0)


def _swiglu_chunk(xb, wgu):
    tf = wgu.shape[1] // 2
    gu = jnp.dot(xb, wgu, preferred_element_type=F32)
    gate, up = gu[:, :tf], gu[:, tf:]
    return (gate * jax.nn.sigmoid(gate) * up).astype(BF16)


class _CastJob(NamedTuple):
    src: jax.Array
    block: tuple
    start: int
    panel: int = 0

    @property
    def n_blocks(self):
        return (self.src.shape[0] // self.block[0]) * (self.src.shape[1] // self.block[1])

    @property
    def out_shape(self):
        rows, cols = self.src.shape
        col_blocks = cols // self.block[1]
        shape = (self.block[1] // self.panel, rows, col_blocks * self.panel) if self.panel else (rows, cols)
        return jax.ShapeDtypeStruct(shape, BF16)

    def _block_index(self, steps_per_row_tile, i, j):
        col_blocks = self.src.shape[1] // self.block[1]
        s = jnp.clip(i * steps_per_row_tile + j - self.start, 0, self.n_blocks - 1)
        return s // col_blocks, s % col_blocks

    def in_spec(self, steps_per_row_tile):
        return pl.BlockSpec(self.block, functools.partial(self._block_index, steps_per_row_tile))

    def out_spec(self, steps_per_row_tile):
        if not self.panel:
            return self.in_spec(steps_per_row_tile)
        br, bc = self.block

        def index_map(i, j):
            rb, cb = self._block_index(steps_per_row_tile, i, j)
            return (0, rb, cb)

        return pl.BlockSpec((bc // self.panel, br, self.panel), index_map)


def _cast_block(src_ref, dst_ref):
    if len(dst_ref.shape) == 2:
        dst_ref[...] = src_ref[...].astype(BF16)
    else:
        panel = dst_ref.shape[2]
        for p in range(dst_ref.shape[0]):
            dst_ref[p] = src_ref[:, p * panel:(p + 1) * panel].astype(BF16)


def _ffn_kernel(*refs, n_tiles, tm, cast_ranges):
    n_cast = len(cast_ranges)
    n_fixed = 6
    xres_ref, xnext_ref, wgu_ref, wd_ref, g_ref, b_ref = refs[:n_fixed]
    cast_in = refs[n_fixed:n_fixed + n_cast]
    n_in = len(refs) - (5 + n_cast)
    tail_hbm = refs[n_fixed + n_cast] if n_in > n_fixed + n_cast else None
    out_hbm = refs[n_in]
    cast_out = refs[n_in + 1:n_in + 1 + n_cast]
    acc_ref, xb_ref, sem, tail_sem = refs[n_in + 1 + n_cast:]
    i = pl.program_id(0)
    j = pl.program_id(1)
    nf = pl.num_programs(1)
    d = acc_ref.shape[1]
    n_slabs = d // X_SLAB_COLS
    slab = pl.ds(pl.multiple_of(jnp.minimum(j, n_slabs - 1) * X_SLAB_COLS, X_SLAB_COLS), X_SLAB_COLS)

    if tail_hbm is not None:
        tail_copy = pltpu.make_async_copy(
            tail_hbm, out_hbm.at[pl.ds(n_tiles * tm, tail_hbm.shape[0])], tail_sem)

        @pl.when(jnp.logical_and(i == 0, j == 0))
        def _():
            tail_copy.start()

        @pl.when(jnp.logical_and(i == n_tiles, j == nf - 1))
        def _():
            tail_copy.wait()

    @pl.when(jnp.logical_and(i < n_tiles, j < n_slabs))
    def _():
        xb_ref.at[i % 2][:, slab] = xnext_ref[...].astype(BF16)

    step = (i - 1) * nf + j
    for src_ref, dst_ref, (start, stop) in zip(cast_in, cast_out, cast_ranges):
        @pl.when(jnp.logical_and(step >= start, step < stop))
        def _(src_ref=src_ref, dst_ref=dst_ref):
            _cast_block(src_ref, dst_ref)

    @pl.when(i > 0)
    def _():
        @pl.when(j == 0)
        def _():
            acc_ref[...] = jnp.zeros_like(acc_ref)

        @pl.when(j < n_slabs)
        def _():
            acc_ref[:, slab] += (ALPHA / 0.5) * xres_ref[...]

        act = _swiglu_chunk(xb_ref[(i - 1) % 2], wgu_ref[...])
        for n0 in range(0, d, FFN_DOWN_COLS):
            cols = slice(n0, n0 + FFN_DOWN_COLS)
            acc_ref[:, cols] += jnp.dot(act, wd_ref[:, cols], preferred_element_type=F32)

        @pl.when(j == nf - 1)
        def _():
            _layer_norm_and_write_back(acc_ref, out_hbm, sem, (i - 1) * tm, 0.5, g_ref, b_ref)


def _layer_norm_and_write_back(acc_ref, out_hbm, sem, row0, branch_scale, g_ref, b_ref):
    rows = acc_ref.shape[0]
    group = _ln_group_rows(rows)
    n_groups = rows // group
    in_flight = min(OUT_COPIES_IN_FLIGHT, n_groups)

    def group_copy(k):
        r0 = pl.multiple_of(k * group, group)
        return pltpu.make_async_copy(
            acc_ref.at[pl.ds(r0, group)], out_hbm.at[pl.ds(row0 + r0, group)], sem.at[k % in_flight])

    def body(k, carry):
        @pl.when(k >= in_flight)
        def _():
            group_copy(k - in_flight).wait()

        _layer_norm_group(acc_ref, pl.multiple_of(k * group, group), None, branch_scale, g_ref, b_ref)
        group_copy(k).start()
        return carry

    lax.fori_loop(0, n_groups, body, 0)
    for k in range(n_groups - in_flight, n_groups):
        group_copy(k).wait()


def _ffn_ln(x, wgu, w_down, ln_g, ln_b, *, tm, tail=None, cast_jobs=()):
    m, d = x.shape
    f = w_down.shape[0]
    tf = wgu.shape[2] // 2
    n_tiles = (m - (0 if tail is None else tail.shape[0])) // tm
    assert n_tiles * tm + (0 if tail is None else tail.shape[0]) == m
    assert f % tf == 0 and tf == V7X_MXU_COLS and tm % _ln_group_rows(tm) == 0
    nf = f // tf
    n_slabs = d // X_SLAB_COLS
    assert n_slabs <= nf
    for job in cast_jobs:
        assert job.start + job.n_blocks <= n_tiles * nf

    def weight_chunk(i, j):
        return jnp.where(i == 0, 0, j)

    def slab_col(j):
        return jnp.minimum(j, n_slabs - 1)

    shifted_jobs = [job._replace(start=job.start + nf) for job in cast_jobs]
    operands = [x, x, wgu, w_down, ln_g, ln_b] + [job.src for job in cast_jobs]
    if tail is not None:
        operands.append(tail)
    outs = pl.pallas_call(
        functools.partial(
            _ffn_kernel, n_tiles=n_tiles, tm=tm,
            cast_ranges=tuple((job.start, job.start + job.n_blocks) for job in cast_jobs)),
        grid=(n_tiles + 1, nf),
        in_specs=[
            pl.BlockSpec((tm, X_SLAB_COLS), lambda i, j: (jnp.maximum(i - 1, 0), slab_col(j))),
            pl.BlockSpec((tm, X_SLAB_COLS), lambda i, j: (jnp.minimum(i, n_tiles - 1), slab_col(j))),
            pl.BlockSpec((None, d, 2 * tf), lambda i, j: (weight_chunk(i, j), 0, 0)),
            pl.BlockSpec((tf, d), lambda i, j: (weight_chunk(i, j), 0)),
            pl.BlockSpec((1, d), lambda i, j: (0, 0)),
            pl.BlockSpec((1, d), lambda i, j: (0, 0)),
        ] + [job.in_spec(nf) for job in shifted_jobs]
        + ([pl.BlockSpec(memory_space=pl.ANY)] if tail is not None else []),
        out_specs=[pl.BlockSpec(memory_space=pl.ANY)] + [job.out_spec(nf) for job in shifted_jobs],
        out_shape=[jax.ShapeDtypeStruct((m, d), F32)] + [job.out_shape for job in cast_jobs],
        scratch_shapes=[
            pltpu.VMEM((tm, d), F32),
            pltpu.VMEM((2, tm, d), BF16),
            pltpu.SemaphoreType.DMA((OUT_COPIES_IN_FLIGHT,)),
            pltpu.SemaphoreType.DMA(()),
        ],
        compiler_params=pltpu.CompilerParams(
            dimension_semantics=("arbitrary", "arbitrary"),
            vmem_limit_bytes=V7X_VMEM_LIMIT_BYTES,
        ),
        name="ffn_ln",
    )(*operands)
    return outs[0], tuple(outs[1:])


def _ffn_nested_kernel(x_hbm, wgu_hbm, wd_hbm, g_hbm, b_hbm, out_hbm, acc_ref, xb_ref, sem,
                       *, n_tiles, tm, nf):
    d = acc_ref.shape[1]
    tf = wd_hbm.shape[0] // nf
    n_slabs = d // X_SLAB_COLS

    def step(xres_ref, xnext_ref, wgu_ref, wd_ref, g_ref, b_ref):
        i = pl.program_id(0)
        j = pl.program_id(1)
        slab = pl.ds(
            pl.multiple_of(jnp.minimum(j, n_slabs - 1) * X_SLAB_COLS, X_SLAB_COLS), X_SLAB_COLS)

        @pl.when(jnp.logical_and(i < n_tiles, j < n_slabs))
        def _():
            xb_ref.at[i % 2][:, slab] = xnext_ref[...].astype(BF16)

        @pl.when(i > 0)
        def _():
            @pl.when(j == 0)
            def _():
                acc_ref[...] = jnp.zeros_like(acc_ref)

            @pl.when(j < n_slabs)
            def _():
                acc_ref[:, slab] += (ALPHA / 0.5) * xres_ref[...]

            act = _swiglu_chunk(xb_ref[(i - 1) % 2], wgu_ref[...])
            for n0 in range(0, d, FFN_DOWN_COLS):
                cols = slice(n0, n0 + FFN_DOWN_COLS)
                acc_ref[:, cols] += jnp.dot(act, wd_ref[:, cols], preferred_element_type=F32)

            @pl.when(j == nf - 1)
            def _():
                _layer_norm_and_write_back(acc_ref, out_hbm, sem, (i - 1) * tm, 0.5, g_ref, b_ref)

    def weight_chunk(i, j):
        return jnp.where(i == 0, 0, j)

    def slab_col(j):
        return jnp.minimum(j, n_slabs - 1)

    pltpu.emit_pipeline(
        step,
        grid=(n_tiles + 1, nf),
        in_specs=[
            pl.BlockSpec((tm, X_SLAB_COLS), lambda i, j: (jnp.maximum(i - 1, 0), slab_col(j))),
            pl.BlockSpec((tm, X_SLAB_COLS), lambda i, j: (jnp.minimum(i, n_tiles - 1), slab_col(j))),
            pl.BlockSpec((None, d, 2 * tf), lambda i, j: (weight_chunk(i, j), 0, 0),
                         pipeline_mode=pl.Buffered(3)),
            pl.BlockSpec((tf, d), lambda i, j: (weight_chunk(i, j), 0), pipeline_mode=pl.Buffered(3)),
            pl.BlockSpec((1, d), lambda i, j: (0, 0)),
            pl.BlockSpec((1, d), lambda i, j: (0, 0)),
        ],
    )(x_hbm, x_hbm, wgu_hbm, wd_hbm, g_hbm, b_hbm)


def _ffn_ln_nested(x, wgu, w_down, ln_g, ln_b, *, tm):
    m, d = x.shape
    f = w_down.shape[0]
    tf = wgu.shape[2] // 2
    assert m % tm == 0 and f % tf == 0 and tf == V7X_MXU_COLS and tm % _ln_group_rows(tm) == 0
    nf = f // tf
    assert d // X_SLAB_COLS <= nf
    any_space = pl.BlockSpec(memory_space=pl.ANY)
    return pl.pallas_call(
        functools.partial(_ffn_nested_kernel, n_tiles=m // tm, tm=tm, nf=nf),
        in_specs=[any_space] * 5,
        out_specs=any_space,
        out_shape=jax.ShapeDtypeStruct((m, d), F32),
        scratch_shapes=[
            pltpu.VMEM((tm, d), F32),
            pltpu.VMEM((2, tm, d), BF16),
            pltpu.SemaphoreType.DMA((OUT_COPIES_IN_FLIGHT,)),
        ],
        compiler_params=pltpu.CompilerParams(vmem_limit_bytes=V7X_VMEM_LIMIT_BYTES),
        name="ffn_ln_nested",
    )(x, wgu, w_down, ln_g, ln_b)


def _ffn_head_kernel(x_ref, meta_ref, wg_ref, wu_ref, wd_ref, g_ref, b_ref,
                     o_ref, om_ref, wgub_ref, wdb_ref, xb_ref):
    j = pl.program_id(0)
    n_meta = meta_ref.shape[0]

    @pl.when(j == 0)
    def _():
        xb_ref[0:n_meta, :] = meta_ref[...].astype(BF16)
        xb_ref[n_meta:, :] = x_ref[...].astype(BF16)
        o_ref[...] = jnp.zeros_like(o_ref)
        om_ref[...] = jnp.zeros_like(om_ref)

    tf = wg_ref.shape[1]
    wgub_ref[:, 0:tf] = wg_ref[...].astype(BF16)
    wgub_ref[:, tf:] = wu_ref[...].astype(BF16)
    wdb_ref[...] = wd_ref[...].astype(BF16)
    act = _swiglu_chunk(xb_ref[...], wgub_ref[...])
    d = o_ref.shape[1]
    for n0 in range(0, d, FFN_DOWN_COLS):
        cols = slice(n0, n0 + FFN_DOWN_COLS)
        part = jnp.dot(act, wdb_ref[:, cols], preferred_element_type=F32)
        om_ref[:, cols] += part[0:n_meta, :]
        o_ref[:, cols] += part[n_meta:, :]

    @pl.when(j == pl.num_programs(0) - 1)
    def _():
        _residual_layer_norm(om_ref, meta_ref, 0.5, g_ref, b_ref)
        _residual_layer_norm(o_ref, x_ref, 0.5, g_ref, b_ref)


def _ffn_ln_head(x, meta, w_gu, w_down, ln_g, ln_b, *, tm):
    m, d = x.shape
    n_meta = meta.shape[0]
    f = w_down.shape[0]
    tf = V7X_MXU_COLS
    nf = f // tf
    assert m % tm == 0
    last = m // tm - 1
    resident = pl.Buffered(1)
    return pl.pallas_call(
        _ffn_head_kernel,
        grid=(nf,),
        in_specs=[
            pl.BlockSpec((tm, d), lambda j: (last, 0), pipeline_mode=resident),
            pl.BlockSpec((n_meta, d), lambda j: (0, 0), pipeline_mode=resident),
            pl.BlockSpec((d, tf), lambda j: (0, j)),
            pl.BlockSpec((d, tf), lambda j: (0, nf + j)),
            pl.BlockSpec((tf, d), lambda j: (j, 0)),
            pl.BlockSpec((1, d), lambda j: (0, 0)),
            pl.BlockSpec((1, d), lambda j: (0, 0)),
        ],
        out_specs=[
            pl.BlockSpec((tm, d), lambda j: (0, 0), pipeline_mode=resident),
            pl.BlockSpec((n_meta, d), lambda j: (0, 0), pipeline_mode=resident),
            pl.BlockSpec((None, d, 2 * tf), lambda j: (j, 0, 0)),
            pl.BlockSpec((tf, d), lambda j: (j, 0)),
        ],
        out_shape=[
            jax.ShapeDtypeStruct((tm, d), F32),
            jax.ShapeDtypeStruct((n_meta, d), F32),
            jax.ShapeDtypeStruct((nf, d, 2 * tf), BF16),
            jax.ShapeDtypeStruct((f, d), BF16),
        ],
        scratch_shapes=[pltpu.VMEM((n_meta + tm, d), BF16)],
        compiler_params=pltpu.CompilerParams(
            dimension_semantics=("arbitrary",),
            vmem_limit_bytes=V7X_VMEM_LIMIT_BYTES,
        ),
        name="ffn_ln_head",
    )(x, meta, w_gu, w_gu, w_down, ln_g, ln_b)


def _mixer_kernel(hres_ref, hnext_ref, halo_ref, meta_ref, wa_ref, wb_ref, wc_ref, wo_ref, pw_ref,
                  cw_ref, ps_ref, g_ref, b_ref, out_hbm, acc_ref, xb_ref, yp_ref, sem,
                  *, n_tiles, ts, tiles_per_seq, n_conv):
    i = pl.program_id(0)
    c = pl.program_id(1)
    d = acc_ref.shape[1]
    tc = wa_ref.shape[1]
    n_slabs = d // MIXER_SLAB_COLS
    slab = pl.ds(
        pl.multiple_of(jnp.minimum(c, n_slabs - 1) * MIXER_SLAB_COLS, MIXER_SLAB_COLS), MIXER_SLAB_COLS)

    def round_next_slab():
        first = (i % tiles_per_seq) == 0
        nxt = xb_ref.at[i % 2]
        nxt[0:HALO, slab] = jnp.where(first, meta_ref[...], halo_ref[...]).astype(BF16)
        nxt[HALO:, slab] = hnext_ref[...].astype(BF16)

    def accumulate(y):
        for n0 in range(0, d, FFN_DOWN_COLS):
            cols = slice(n0, n0 + FFN_DOWN_COLS)
            acc_ref[:, cols] += jnp.dot(y, wo_ref[:, cols], preferred_element_type=F32)

    @pl.when(jnp.logical_and(i < n_tiles, c < n_slabs))
    def _():
        round_next_slab()

    @pl.when(i > 0)
    def _():
        @pl.when(c == 0)
        def _():
            acc_ref[...] = jnp.zeros_like(acc_ref)

        @pl.when(c < n_slabs)
        def _():
            acc_ref[:, slab] += ALPHA * hres_ref[...]

        pool_step = c - n_conv
        group = pool_step // 2

        @pl.when(c < n_conv)
        def _():
            half = HALO + (ts // 2)
            hb_ref = xb_ref.at[(i - 1) % 2]
            tops = [jnp.dot(hb_ref[0:half, :], w[...], preferred_element_type=F32)
                    for w in (wa_ref, wb_ref, wc_ref)]
            bots = [jnp.dot(hb_ref[half:, :], w[...], preferred_element_type=F32)
                    for w in (wa_ref, wb_ref, wc_ref)]
            gate_b, gate_c, x_in = [jnp.concatenate(tb, axis=0) for tb in zip(tops, bots)]
            v = gate_c * x_in
            conv = pltpu.roll(v, 2, 0) * cw_ref[0:1, :]
            conv = conv + pltpu.roll(v, 1, 0) * cw_ref[1:2, :]
            conv = conv + v * cw_ref[2:3, :]
            accumulate((gate_b * conv)[HALO:, :].astype(BF16))

        @pl.when(jnp.logical_and(pool_step >= 0, pool_step % 2 == 0))
        def _():
            hb = xb_ref[(i - 1) % 2]
            z = jnp.concatenate([jnp.dot(hb, wa_ref[...], preferred_element_type=F32),
                                 jnp.dot(hb, wb_ref[...], preferred_element_type=F32)], axis=1)
            s = z
            for level in range(len(POOL_WINDOWS)):
                doubled = s + pltpu.roll(s, 1 << level, 0)
                s = doubled if level == 0 else jnp.where(level <= group, doubled, s)
            inv_w = jnp.where(group == 0, 1.0 / POOL_WINDOWS[0],
                              jnp.where(group == 1, 1.0 / POOL_WINDOWS[1],
                                        jnp.where(group == 2, 1.0 / POOL_WINDOWS[2],
                                                  1.0 / POOL_WINDOWS[3])))
            diff = (s * inv_w.astype(F32) - z)[HALO:, :].astype(BF16)
            y = (jnp.dot(diff, pw_ref[...], preferred_element_type=F32) * ps_ref[...]).astype(BF16)
            yp_ref[...] = y
            accumulate(y[:, 0:tc])

        @pl.when(jnp.logical_and(pool_step >= 0, pool_step % 2 == 1))
        def _():
            accumulate(yp_ref[:, tc:])

        @pl.when(c == pl.num_programs(1) - 1)
        def _():
            _layer_norm_and_write_back(acc_ref, out_hbm, sem, (i - 1) * ts, None, g_ref, b_ref)


def _mixer_ln(h, meta_h, w_in, conv_w, pool_w, pool_scale, w_out, ln_g, ln_b, *, ts, seq):
    m, d = h.shape
    cc = conv_w.shape[1]
    n_groups, pg, _ = pool_w.shape
    n_panels, _, tc = w_in.shape
    n_conv = cc // tc
    assert pg == 2 * tc and n_panels == 3 * n_conv + 2 * n_groups and w_out.shape[0] == cc + n_groups * pg
    n_steps = n_conv + 2 * n_groups
    n_tiles = m // ts
    tiles_per_seq = seq // ts
    n_slabs = d // MIXER_SLAB_COLS
    assert m % ts == 0 and seq % ts == 0 and n_slabs <= n_steps and ts % _ln_group_rows(ts) == 0

    def step(i, c):
        return jnp.where(i == 0, 0, c)

    def group(i, c):
        return jnp.maximum(step(i, c) - n_conv, 0) // 2

    def slab_col(c):
        return jnp.minimum(c, n_slabs - 1)

    def next_tile(i):
        return jnp.minimum(i, n_tiles - 1)

    def pool_or(i, c, conv_panel, pool_panel):
        return jnp.where(step(i, c) < n_conv, conv_panel, pool_panel)

    return pl.pallas_call(
        functools.partial(
            _mixer_kernel, n_tiles=n_tiles, ts=ts, tiles_per_seq=tiles_per_seq, n_conv=n_conv),
        grid=(n_tiles + 1, n_steps),
        in_specs=[
            pl.BlockSpec((ts, MIXER_SLAB_COLS), lambda i, c: (jnp.maximum(i - 1, 0), slab_col(c))),
            pl.BlockSpec((ts, MIXER_SLAB_COLS), lambda i, c: (next_tile(i), slab_col(c))),
            pl.BlockSpec((HALO, MIXER_SLAB_COLS),
                         lambda i, c: (jnp.maximum(next_tile(i) * (ts // HALO) - 1, 0), slab_col(c))),
            pl.BlockSpec((HALO, MIXER_SLAB_COLS), lambda i, c: (0, slab_col(c))),
            pl.BlockSpec((None, d, tc), lambda i, c: (
                pool_or(i, c, step(i, c), 3 * n_conv + 2 * group(i, c)), 0, 0)),
            pl.BlockSpec((None, d, tc), lambda i, c: (
                pool_or(i, c, n_conv + step(i, c), 3 * n_conv + 2 * group(i, c) + 1), 0, 0)),
            pl.BlockSpec((None, d, tc), lambda i, c: (
                2 * n_conv + jnp.minimum(step(i, c), n_conv - 1), 0, 0)),
            pl.BlockSpec((tc, d), lambda i, c: (step(i, c), 0)),
            pl.BlockSpec((None, pg, pg), lambda i, c: (group(i, c), 0, 0)),
            pl.BlockSpec((CONV_K, tc), lambda i, c: (0, jnp.minimum(step(i, c), n_conv - 1))),
            pl.BlockSpec((1, pg), lambda i, c: (0, group(i, c))),
            pl.BlockSpec((1, d), lambda i, c: (0, 0)),
            pl.BlockSpec((1, d), lambda i, c: (0, 0)),
        ],
        out_specs=pl.BlockSpec(memory_space=pl.ANY),
        out_shape=jax.ShapeDtypeStruct((m, d), F32),
        scratch_shapes=[
            pltpu.VMEM((ts, d), F32),
            pltpu.VMEM((2, HALO + ts, d), BF16),
            pltpu.VMEM((ts, pg), BF16),
            pltpu.SemaphoreType.DMA((OUT_COPIES_IN_FLIGHT,)),
        ],
        compiler_params=pltpu.CompilerParams(
            dimension_semantics=("arbitrary", "arbitrary"),
            vmem_limit_bytes=V7X_VMEM_LIMIT_BYTES,
        ),
        name="mixer_ln",
    )(h, h, h, meta_h, w_in, w_in, w_in, w_out, pool_w, conv_w, pool_scale, ln_g, ln_b)


def kernel(x, meta_tokens, ffn1_w_gu, ffn1_w_down, ln1_g, ln1_b, w_in, conv_w, pool_w, pool_scale,
           w_out, ln2_g, ln2_b, ffn2_w_gu, ffn2_w_down, ln3_g, ln3_b):
    b, seq, d = x.shape
    assert meta_tokens.shape == (N_META, d) and N_META == HALO
    assert ffn1_w_gu.shape[0] == DEPTH

    h = x.reshape(b * seq, d)
    meta = meta_tokens.astype(x.dtype)
    for i in range(DEPTH):
        g1, b1 = ln1_g[i][None].astype(F32), ln1_b[i][None].astype(F32)
        g2, b2 = ln2_g[i][None].astype(F32), ln2_b[i][None].astype(F32)
        g3, b3 = ln3_g[i][None].astype(F32), ln3_b[i][None].astype(F32)
        n_groups, pg, _ = pool_w[i].shape

        h1_head, meta1, wgu1, wd1 = _ffn_ln_head(
            h, meta, ffn1_w_gu[i], ffn1_w_down[i], g1, b1, tm=HEAD_ROWS)
        jobs = (
            _CastJob(ffn2_w_gu[i], (32, ffn2_w_gu[i].shape[1] // 2), 0, panel=V7X_MXU_COLS),
            _CastJob(ffn2_w_down[i], (64, d), 0),
            _CastJob(w_out[i], (64, d), ffn2_w_down[i].shape[0] // 64),
            _CastJob(w_in[i], (32, w_in[i].shape[1]), 0, panel=V7X_MXU_COLS),
            _CastJob(pool_w[i].reshape(n_groups * pg, pg), (128, pg), w_in[i].shape[0] // 32),
        )
        h1, (wgu2, wd2, w_out_b, w_in_b, pool_w_b) = _ffn_ln(
            h, wgu1, wd1, g1, b1, tm=FFN1_TILE_ROWS, tail=h1_head, cast_jobs=jobs)
        pool_w_b = pool_w_b.reshape(n_groups, pg, pg)

        h2 = _mixer_ln(h1, meta1, w_in_b, conv_w[i], pool_w_b, pool_scale[i][None], w_out_b, g2, b2,
                       ts=MIXER_TILE_ROWS, seq=seq)
        h = _ffn_ln_nested(h2, wgu2, wd2, g3, b3, tm=FFN2_TILE_ROWS)
        assert DEPTH == 1
    return h.reshape(b, seq, d)
```
